```python
import math
import numpy as np
import jax
import jax.numpy as jnp
from jax import lax

D_MODEL = 1024
BATCH = 1
SEQ = 16384
DEPTH = 1
DEC_BATCH = 4
DEC_SEQ = 8192
PAST_LEN = 128

HEAD_DIM = 64
N_HEADS_TOTAL = D_MODEL // HEAD_DIM
N_HEADS_M = 4
N_HEADS_A = (N_HEADS_TOTAL - N_HEADS_M) // 2
N_HEADS_B = N_HEADS_TOTAL - N_HEADS_M - N_HEADS_A
WIDTH_A = N_HEADS_A * HEAD_DIM
WIDTH_B = N_HEADS_B * HEAD_DIM
WIDTH_M = N_HEADS_M * HEAD_DIM
MIX_WIDTH = WIDTH_A + WIDTH_B + WIDTH_M
IN_WIDTH = 3 * WIDTH_A + 3 * WIDTH_B + WIDTH_M
DILATED_PATTERNS = ((128, 1), (512, 4), (2048, 16))
GRID_W = 64
NA_ROWS = 8
NA_COLS = 16
N_MEM = 256
D_FF = 4 * D_MODEL
ROPE_THETA = 500000.0
ROPE_DIMS = HEAD_DIM // 4
EPS = 1e-6
NEG_INF = -1e30

kernel_name = "hybrid_dilated_natten_memory_encoder"


def rmsnorm(x, g):
    xf = x.astype(jnp.float32)
    y = xf * lax.rsqrt(jnp.mean(xf * xf, axis=-1, keepdims=True) + EPS)
    return (y * g.astype(jnp.float32)).astype(x.dtype)


def partial_rope(x, pos):
    half = ROPE_DIMS // 2
    inv = ROPE_THETA ** (-(jnp.arange(half, dtype=jnp.float32) * 2.0 / ROPE_DIMS))
    ang = pos[:, None] * inv[None, :]
    cos = jnp.cos(ang)[None, :, None, :]
    sin = jnp.sin(ang)[None, :, None, :]
    xf = x.astype(jnp.float32)
    x1 = xf[..., :half]
    x2 = xf[..., half:ROPE_DIMS]
    rot = jnp.concatenate([x1 * cos - x2 * sin, x2 * cos + x1 * sin, xf[..., ROPE_DIMS:]], axis=-1)
    return rot.astype(x.dtype)


def local_window_attention(q, k, v, half):
    n, L, h, d = q.shape
    bq = half
    nb = -(-L // bq)
    lp = nb * bq
    qp = jnp.pad(q, ((0, 0), (0, lp - L), (0, 0), (0, 0))).reshape(n, nb, bq, h, d)
    pad_kv = ((0, 0), (half, lp - L + half), (0, 0), (0, 0))
    kp = jnp.pad(k, pad_kv).reshape(n, nb + 2, bq, h, d)
    vp = jnp.pad(v, pad_kv).reshape(n, nb + 2, bq, h, d)
    kb = jnp.concatenate([kp[:, :nb], kp[:, 1:nb + 1], kp[:, 2:nb + 2]], axis=2)
    vb = jnp.concatenate([vp[:, :nb], vp[:, 1:nb + 1], vp[:, 2:nb + 2]], axis=2)
    qpos = np.arange(lp).reshape(nb, bq)
    kpos = np.arange(nb)[:, None] * bq + np.arange(3 * bq)[None, :] - half
    dist = kpos[:, None, :] - qpos[:, :, None]
    mask = (np.abs(dist) <= half) & (kpos[:, None, :] >= 0) & (kpos[:, None, :] < L)
    s = jnp.einsum('nbqhd,nbkhd->nhbqk', qp, kb).astype(jnp.float32) * (d ** -0.5)
    s = jnp.where(mask[None, None], s, NEG_INF)
    lse = jax.nn.logsumexp(s, axis=-1)
    p = jnp.exp(s - lse[..., None]).astype(v.dtype)
    o = jnp.einsum('nhbqk,nbkhd->nbqhd', p, vb).reshape(n, lp, h, d)[:, :L]
    lse = lse.transpose(0, 2, 3, 1).reshape(n, lp, h)[:, :L]
    return o, lse


def dilated_attention(q, k, v):
    b, t, h, d = q.shape
    outs = []
    lses = []
    for window, dil in DILATED_PATTERNS:
        half = window // (2 * dil)
        L = t // dil
        def to_res(z):
            return z.reshape(b, L, dil, h, d).transpose(0, 2, 1, 3, 4).reshape(b * dil, L, h, d)
        o, lse = local_window_attention(to_res(q), to_res(k), to_res(v), half)
        outs.append(o.reshape(b, dil, L, h, d).transpose(0, 2, 1, 3, 4).reshape(b, t, h, d))
        lses.append(lse.reshape(b, dil, L, h).transpose(0, 2, 1, 3).reshape(b, t, h))
    w = jax.nn.softmax(jnp.stack(lses, axis=0), axis=0)
    o = jnp.sum(w[..., None] * jnp.stack(outs, axis=0).astype(jnp.float32), axis=0)
    return o.astype(q.dtype)


def neighbourhood_attention(q, k, v, rpb):
    b, t, h, d = q.shape
    rows = t // GRID_W
    kr = min(NA_ROWS, rows)
    row_start = np.clip(np.arange(rows) - kr // 2, 0, rows - kr)
    col_start = np.clip(np.arange(GRID_W) - NA_COLS // 2, 0, GRID_W - NA_COLS)
    col_idx = col_start[:, None] + np.arange(NA_COLS)[None, :]
    dcol = col_idx - np.arange(GRID_W)[:, None] + NA_COLS - 1
    drow = row_start[:, None] + np.arange(kr)[None, :] - np.arange(rows)[:, None] + NA_ROWS - 1
    bias_col = rpb[:, :, dcol]
    qg = q.reshape(b, rows, GRID_W, h, d)
    kg = k.reshape(b, rows, GRID_W, h, d)
    vg = v.reshape(b, rows, GRID_W, h, d)
    scale = d ** -0.5

    def one_row(args):
        q_row, r0, dr = args
        k_rows = lax.dynamic_slice_in_dim(kg, r0, kr, axis=1)
        v_rows = lax.dynamic_slice_in_dim(vg, r0, kr, axis=1)
        k_win = k_rows[:, :, col_idx]
        v_win = v_rows[:, :, col_idx]
        s = jnp.einsum('bchd,bicjhd->bhcij', q_row, k_win).astype(jnp.float32) * scale
        bias = bias_col[:, dr].transpose(0, 2, 1, 3).astype(jnp.float32)
        s = s + bias[None]
        p = jax.nn.softmax(s.reshape(b, h, GRID_W, kr * NA_COLS), axis=-1)
        p = p.reshape(b, h, GRID_W, kr, NA_COLS).astype(v.dtype)
        return jnp.einsum('bhcij,bicjhd->bchd', p, v_win)

    out = lax.map(one_row, (qg.transpose(1, 0, 2, 3, 4),
                            jnp.asarray(row_start, dtype=jnp.int32),
                            jnp.asarray(drow, dtype=jnp.int32)))
    return out.transpose(1, 0, 2, 3, 4).reshape(b, t, h, d)


def memory_attention(q, mem, norm_mem, w_mem_kv, k_norm_m):
    b, m, _ = mem.shape
    kv = (rmsnorm(mem, norm_mem) @ w_mem_kv).reshape(b, m, 2, N_HEADS_M, HEAD_DIM)
    k = rmsnorm(kv[:, :, 0], k_norm_m)
    v = kv[:, :, 1]
    s = jnp.einsum('bthd,bmhd->bhtm', q, k).astype(jnp.float32) * (HEAD_DIM ** -0.5)
    p = jax.nn.softmax(s, axis=-1).astype(v.dtype)
    return jnp.einsum('bhtm,bmhd->bthd', p, v)


def encoder_layer(x, mem, norm_attn, w_in, q_norm_a, k_norm_a, q_norm_b, k_norm_b, rpb_b,
                  norm_mem, w_mem_kv, q_norm_m, k_norm_m, out_norm_a, out_norm_b, out_norm_m,
                  w_out, norm_ffn, w_ff1, w_ff2):
    b, t, _ = x.shape
    h = rmsnorm(x, norm_attn)
    proj = h @ w_in
    splits = np.cumsum([WIDTH_A, WIDTH_A, WIDTH_A, WIDTH_B, WIDTH_B, WIDTH_B])
    qa, ka, va, qb, kb, vb, qm = jnp.split(proj, splits, axis=-1)
    heads = lambda z, n: z.reshape(b, t, n, HEAD_DIM)
    pos = jnp.arange(t, dtype=jnp.float32)
    qa = partial_rope(rmsnorm(heads(qa, N_HEADS_A), q_norm_a), pos)
    ka = partial_rope(rmsnorm(heads(ka, N_HEADS_A), k_norm_a), pos)
    oa = dilated_attention(qa, ka, heads(va, N_HEADS_A))
    qb = rmsnorm(heads(qb, N_HEADS_B), q_norm_b)
    kb = rmsnorm(heads(kb, N_HEADS_B), k_norm_b)
    ob = neighbourhood_attention(qb, kb, heads(vb, N_HEADS_B), rpb_b)
    qm = rmsnorm(heads(qm, N_HEADS_M), q_norm_m)
    om = memory_attention(qm, mem, norm_mem, w_mem_kv, k_norm_m)
    mixed = jnp.concatenate([rmsnorm(oa.reshape(b, t, WIDTH_A), out_norm_a),
                             rmsnorm(ob.reshape(b, t, WIDTH_B), out_norm_b),
                             rmsnorm(om.reshape(b, t, WIDTH_M), out_norm_m)], axis=-1)
    x = x + mixed @ w_out
    hf = rmsnorm(x, norm_ffn)
    x = x + jnp.square(jax.nn.relu(hf @ w_ff1)) @ w_ff2
    return x


def setup_inputs(seed: int = 0) -> dict:
    key = jax.random.key(seed)
    ks = jax.random.split(key, 24)
    f32 = jnp.float32
    nrm = lambda k, shape, scale: jax.random.normal(k, shape, dtype=f32) * scale
    gain = lambda k, n: 1.0 + 0.02 * jax.random.normal(k, (DEPTH, n), dtype=f32)
    return {
        "x_prompt": nrm(ks[0], (BATCH, SEQ, D_MODEL), 1.0),
        "x_sample": nrm(ks[1], (DEC_BATCH, DEC_SEQ, D_MODEL), 1.0),
        "mem_prompt": nrm(ks[2], (BATCH, N_MEM, D_MODEL), 1.0),
        "mem_sample": nrm(ks[3], (DEC_BATCH, N_MEM, D_MODEL), 1.0),
        "norm_attn": gain(ks[4], D_MODEL),
        "w_in": nrm(ks[5], (DEPTH, D_MODEL, IN_WIDTH), D_MODEL ** -0.5),
        "q_norm_a": gain(ks[6], HEAD_DIM),
        "k_norm_a": gain(ks[7], HEAD_DIM),
        "q_norm_b": gain(ks[8], HEAD_DIM),
        "k_norm_b": gain(ks[9], HEAD_DIM),
        "rpb_b": nrm(ks[10], (DEPTH, N_HEADS_B, 2 * NA_ROWS - 1, 2 * NA_COLS - 1), 0.1),
        "norm_mem": gain(ks[11], D_MODEL),
        "w_mem_kv": nrm(ks[12], (DEPTH, D_MODEL, 2 * WIDTH_M), D_MODEL ** -0.5),
        "q_norm_m": gain(ks[13], HEAD_DIM),
        "k_norm_m": gain(ks[14], HEAD_DIM),
        "out_norm_a": gain(ks[15], WIDTH_A),
        "out_norm_b": gain(ks[16], WIDTH_B),
        "out_norm_m": gain(ks[17], WIDTH_M),
        "w_out": nrm(ks[18], (DEPTH, MIX_WIDTH, D_MODEL), MIX_WIDTH ** -0.5),
        "norm_ffn": gain(ks[19], D_MODEL),
        "w_ff1": nrm(ks[20], (DEPTH, D_MODEL, D_FF), D_MODEL ** -0.5),
        "w_ff2": nrm(ks[21], (DEPTH, D_FF, D_MODEL), D_FF ** -0.5),
    }


def reference(x_prompt, x_sample, mem_prompt, mem_sample, norm_attn, w_in, q_norm_a, k_norm_a,
              q_norm_b, k_norm_b, rpb_b, norm_mem, w_mem_kv, q_norm_m, k_norm_m,
              out_norm_a, out_norm_b, out_norm_m, w_out, norm_ffn, w_ff1, w_ff2):
    y_prompt = x_prompt
    y_sample = x_sample
    for i in range(DEPTH):
        layer_args = (norm_attn[i], w_in[i], q_norm_a[i], k_norm_a[i], q_norm_b[i], k_norm_b[i],
                      rpb_b[i], norm_mem[i], w_mem_kv[i], q_norm_m[i], k_norm_m[i],
                      out_norm_a[i], out_norm_b[i], out_norm_m[i], w_out[i], norm_ffn[i],
                      w_ff1[i], w_ff2[i])
        y_prompt = encoder_layer(y_prompt, mem_prompt, *layer_args)
        y_sample = encoder_layer(y_sample, mem_sample, *layer_args)
    return (y_prompt, y_sample)
```

```python
import functools

import numpy as np
import jax
import jax.numpy as jnp
from jax import lax
from jax.experimental import pallas as pl
from jax.experimental.pallas import tpu as pltpu

D_MODEL = 1024
HEAD_DIM = 64
N_HEADS_A = 6
N_HEADS_B = 6
N_HEADS_M = 4
WIDTH_A = N_HEADS_A * HEAD_DIM
WIDTH_B = N_HEADS_B * HEAD_DIM
WIDTH_M = N_HEADS_M * HEAD_DIM
IN_WIDTH = 3 * WIDTH_A + 3 * WIDTH_B + WIDTH_M
DILATIONS = (1, 4, 16)
HALF = 64
GRID_W = 64
NA_ROWS = 8
NA_COLS = 16
N_MEM = 256
D_FF = 4 * D_MODEL
ROPE_THETA = 500000.0
ROPE_DIMS = HEAD_DIM // 4
EPS = 1e-6
NEG = -1e30

LANES = 128
PAIR = 2 * HEAD_DIM
VMEM_LIMIT = 56 * 1024 * 1024

BF16 = jnp.bfloat16
F32 = jnp.float32


def _cparams(n_axes):
    return pltpu.CompilerParams(dimension_semantics=("parallel",) * n_axes,
                                vmem_limit_bytes=VMEM_LIMIT)


def _const_spec(shape):
    nd = len(shape)
    return pl.BlockSpec(shape, lambda *_: (0,) * nd, pipeline_mode=pl.Buffered(1))


def _rms(x, gain):
    return x * lax.rsqrt(jnp.mean(x * x, axis=-1, keepdims=True) + EPS) * gain


def _head_rms(z, bd, gain):
    z2 = z * z
    hi = z2.astype(BF16)
    lo = (z2 - hi.astype(F32)).astype(BF16)
    ss = (jnp.dot(hi, bd, preferred_element_type=F32) + jnp.dot(lo, bd, preferred_element_type=F32))
    return z * lax.rsqrt(ss * (1.0 / HEAD_DIM) + EPS) * gain


def _attend_pair(qp, kp, vaug, bias):
    m_rows = qp.shape[0]
    lane = lax.broadcasted_iota(jnp.int32, qp.shape, 1)
    zero = jnp.zeros_like(qp)
    lhs = jnp.concatenate([jnp.where(lane < HEAD_DIM, qp, zero),
                           jnp.where(lane >= HEAD_DIM, qp, zero)], axis=0)
    s = lax.dot_general(lhs, kp, (((1,), (1,)), ((), ())), preferred_element_type=F32)
    if bias is not None:
        s = s + bias
    mx = jnp.max(s, axis=-1, keepdims=True)
    p = jnp.exp(s - mx).astype(BF16)
    ov = jnp.dot(p, vaug, preferred_element_type=F32)
    first = lax.broadcasted_iota(jnp.int32, (m_rows, LANES), 1) < HEAD_DIM
    o = jnp.where(first, ov[:m_rows, :LANES], ov[m_rows:, :LANES])
    l = jnp.where(first, ov[:m_rows, LANES:], ov[m_rows:, LANES:])
    mm = jnp.where(first, mx[:m_rows], mx[m_rows:])
    return o / l, mm + jnp.log(l)


def _fill_vaug(vaug_ref, row0, v):
    rows = v.shape[0]
    ones = jnp.ones((rows, LANES), BF16)
    for hp in range(v.shape[1] // LANES):
        vaug_ref[row0:row0 + rows, 2 * LANES * hp:2 * LANES * hp + LANES] = v[:, LANES * hp:LANES * (hp + 1)]
        vaug_ref[row0:row0 + rows, 2 * LANES * hp + LANES:2 * LANES * (hp + 1)] = ones


PROJ_TM = 512


def _proj_kernel(x_ref, g_ref, w_ref, bd_ref, gqa_ref, gka_ref, gqb_ref, gkb_ref, gqm_ref,
                 cos_ref, sin_ref,
                 qa_ref, ka_ref, va_ref, qb_ref, kb_ref, vb_ref, qm_ref):
    x = x_ref[...]
    xn = _rms(x, g_ref[...]).astype(BF16)
    proj = jnp.dot(xn, w_ref[...], preferred_element_type=F32)
    bd = bd_ref[...]

    def normed(c0, width, gain_ref):
        parts = []
        for off in range(0, width, 2 * LANES):
            w = min(2 * LANES, width - off)
            parts.append(_head_rms(proj[:, c0 + off:c0 + off + w], bd[:w, :w], gain_ref[:, off:off + w]))
        return jnp.concatenate(parts, axis=1) if len(parts) > 1 else parts[0]

    def rope(y):
        cos = cos_ref[...]
        sin = sin_ref[...]
        lane = lax.broadcasted_iota(jnp.int32, cos.shape, 1) % HEAD_DIM
        low = lane < ROPE_DIMS // 2
        parts = []
        for c in range(0, y.shape[1], LANES):
            yc = y[:, c:c + LANES]
            partner = jnp.where(low, pltpu.roll(yc, LANES - ROPE_DIMS // 2, axis=1),
                                pltpu.roll(yc, ROPE_DIMS // 2, axis=1))
            parts.append(yc * cos + partner * sin)
        return jnp.concatenate(parts, axis=1)

    o = 0
    qa_ref[...] = rope(normed(o, WIDTH_A, gqa_ref)).astype(BF16); o += WIDTH_A
    ka_ref[...] = rope(normed(o, WIDTH_A, gka_ref)).astype(BF16); o += WIDTH_A
    va_ref[...] = proj[:, o:o + WIDTH_A].astype(BF16); o += WIDTH_A
    qb_ref[...] = normed(o, WIDTH_B, gqb_ref).astype(BF16); o += WIDTH_B
    kb_ref[...] = normed(o, WIDTH_B, gkb_ref).astype(BF16); o += WIDTH_B
    vb_ref[...] = proj[:, o:o + WIDTH_B].astype(BF16); o += WIDTH_B
    qm_ref[...] = normed(o, WIDTH_M, gqm_ref).astype(BF16)


def _proj(x, g, w_in, bd, gqa, gka, gqb, gkb, gqm, cos_t, sin_t):
    b, t, _ = x.shape
    tm = PROJ_TM
    tok = lambda width: pl.BlockSpec((None, tm, width), lambda bi, i: (bi, i, 0))
    out_widths = (WIDTH_A, WIDTH_A, WIDTH_A, WIDTH_B, WIDTH_B, WIDTH_B, WIDTH_M)
    return pl.pallas_call(
        _proj_kernel,
        grid=(b, t // tm),
        in_specs=[tok(D_MODEL), _const_spec((1, D_MODEL)), _const_spec((D_MODEL, IN_WIDTH)),
                  _const_spec((2 * LANES, 2 * LANES)),
                  _const_spec((1, WIDTH_A)), _const_spec((1, WIDTH_A)),
                  _const_spec((1, WIDTH_B)), _const_spec((1, WIDTH_B)), _const_spec((1, WIDTH_M)),
                  pl.BlockSpec((tm, LANES), lambda bi, i: (i, 0)),
                  pl.BlockSpec((tm, LANES), lambda bi, i: (i, 0))],
        out_specs=[tok(w) for w in out_widths],
        out_shape=[jax.ShapeDtypeStruct((b, t, w), BF16) for w in out_widths],
        compiler_params=_cparams(2),
        name="proj",
    )(x, g, w_in, bd, gqa, gka, gqb, gkb, gqm, cos_t, sin_t)


def _memkv_kernel(mem_ref, g_ref, w_ref, bd_ref, gk_ref, km_ref, vaug_ref):
    mn = _rms(mem_ref[...], g_ref[...]).astype(BF16)
    kv = jnp.dot(mn, w_ref[...], preferred_element_type=F32)
    km_ref[...] = _head_rms(kv[:, :WIDTH_M], bd_ref[...], gk_ref[...]).astype(BF16)
    _fill_vaug(vaug_ref, 0, kv[:, WIDTH_M:].astype(BF16))


def _memkv(mem, g, w_kv, bd, gk):
    b = mem.shape[0]
    return pl.pallas_call(
        _memkv_kernel,
        grid=(b,),
        in_specs=[pl.BlockSpec((None, N_MEM, D_MODEL), lambda bi: (bi, 0, 0)),
                  _const_spec((1, D_MODEL)), _const_spec((D_MODEL, 2 * WIDTH_M)),
                  _const_spec((2 * LANES, 2 * LANES)), _const_spec((1, WIDTH_M))],
        out_specs=[pl.BlockSpec((None, N_MEM, WIDTH_M), lambda bi: (bi, 0, 0)),
                   pl.BlockSpec((None, N_MEM, 2 * WIDTH_M), lambda bi: (bi, 0, 0))],
        out_shape=[jax.ShapeDtypeStruct((b, N_MEM, WIDTH_M), BF16),
                   jax.ShapeDtypeStruct((b, N_MEM, 2 * WIDTH_M), BF16)],
        compiler_params=_cparams(1),
        name="memkv",
    )(mem, g, w_kv, bd, gk)


DIL_QB = 512
DIL_SUB = 2 * HALF


def _dilated_kernel(q_ref, kp_ref, kc_ref, kn_ref, vp_ref, vc_ref, vn_ref, bias_ref,
                    o_ref, lse_ref, kbuf, vaug, *, n_sub_total):
    qb = q_ref.shape[0]
    kbuf[0:HALF] = kp_ref[...]
    kbuf[HALF:HALF + qb] = kc_ref[...]
    kbuf[HALF + qb:2 * HALF + qb] = kn_ref[...]
    _fill_vaug(vaug, 0, vp_ref[...])
    _fill_vaug(vaug, HALF, vc_ref[...])
    _fill_vaug(vaug, HALF + qb, vn_ref[...])
    n_sub = qb // DIL_SUB
    first_sub = pl.program_id(2) * n_sub

    def body(j, carry):
        gs = first_sub + j
        variant = jnp.where(gs == 0, 0, jnp.where(gs == n_sub_total - 1, 2, 1))
        bias = bias_ref[variant]
        r0 = pl.multiple_of(j * DIL_SUB, DIL_SUB)
        for hp in range(WIDTH_A // LANES):
            o, lse = _attend_pair(q_ref[pl.ds(r0, DIL_SUB), LANES * hp:LANES * (hp + 1)],
                                  kbuf[pl.ds(r0, 2 * DIL_SUB), LANES * hp:LANES * (hp + 1)],
                                  vaug[pl.ds(r0, 2 * DIL_SUB), 2 * LANES * hp:2 * LANES * (hp + 1)],
                                  bias)
            o_ref[pl.ds(r0, DIL_SUB), LANES * hp:LANES * (hp + 1)] = o.astype(BF16)
            lse_ref[pl.ds(r0, DIL_SUB), LANES * hp:LANES * (hp + 1)] = lse
        return carry

    lax.fori_loop(0, n_sub, body, 0)


def _dilated(q, k, v, bias, dil):
    b, t, w = q.shape
    ln = t // dil
    qb = min(DIL_QB, ln)
    view = lambda z: z.reshape(b, ln, dil * w)
    hb = qb // HALF
    n_halo = ln // HALF
    main = pl.BlockSpec((None, qb, w), lambda bi, r, i: (bi, i, r))
    prev = pl.BlockSpec((None, HALF, w), lambda bi, r, i: (bi, jnp.maximum(i * hb - 1, 0), r))
    nxt = pl.BlockSpec((None, HALF, w), lambda bi, r, i: (bi, jnp.minimum((i + 1) * hb, n_halo - 1), r))
    o, lse = pl.pallas_call(
        functools.partial(_dilated_kernel, n_sub_total=ln // DIL_SUB),
        grid=(b, dil, ln // qb),
        in_specs=[main, prev, main, nxt, prev, main, nxt, _const_spec(bias.shape)],
        out_specs=[main, main],
        out_shape=[jax.ShapeDtypeStruct((b, ln, dil * w), BF16),
                   jax.ShapeDtypeStruct((b, ln, dil * w), F32)],
        scratch_shapes=[pltpu.VMEM((qb + 2 * HALF, w), BF16),
                        pltpu.VMEM((qb + 2 * HALF, 2 * w), BF16)],
        compiler_params=_cparams(3),
        name=f"dilated{dil}",
    )(view(q), view(k), view(k), view(k), view(v), view(v), view(v), bias)
    return o.reshape(b, t, w), lse.reshape(b, t, w)


def _band_bias():
    row = np.arange(DIL_SUB)[:, None]
    col = np.arange(2 * DIL_SUB)[None, :]
    band = (col - row >= 0) & (col - row <= 2 * HALF)
    variants = [band & (col >= HALF), band, band & (col < 2 * DIL_SUB - HALF)]
    tab = np.stack([np.where(np.concatenate([m, m], axis=0), 0.0, NEG) for m in variants])
    return jnp.asarray(tab, dtype=F32)


NA_GROUP = 8
NA_TOK = NA_GROUP * GRID_W
NA_KEYS = NA_ROWS * GRID_W


def _natten_kernel(q_ref, kp_ref, kc_ref, kn_ref, vp_ref, vc_ref, vn_ref, bias_ref,
                   o_ref, kbuf, vaug, *, n_rows):
    kbuf[0:NA_TOK] = kp_ref[...]
    kbuf[NA_TOK:2 * NA_TOK] = kc_ref[...]
    kbuf[2 * NA_TOK:3 * NA_TOK] = kn_ref[...]
    _fill_vaug(vaug, 0, vp_ref[...])
    _fill_vaug(vaug, NA_TOK, vc_ref[...])
    _fill_vaug(vaug, 2 * NA_TOK, vn_ref[...])
    g = pl.program_id(1)

    def body(j, carry):
        r = g * NA_GROUP + j
        r0 = jnp.clip(r - NA_ROWS // 2, 0, n_rows - NA_ROWS)
        off = r - r0
        start = pl.multiple_of((r0 - (g - 1) * NA_GROUP) * GRID_W, GRID_W)
        q0 = pl.multiple_of(j * GRID_W, GRID_W)
        for hp in range(WIDTH_B // LANES):
            o, _ = _attend_pair(q_ref[pl.ds(q0, GRID_W), LANES * hp:LANES * (hp + 1)],
                                kbuf[pl.ds(start, NA_KEYS), LANES * hp:LANES * (hp + 1)],
                                vaug[pl.ds(start, NA_KEYS), 2 * LANES * hp:2 * LANES * (hp + 1)],
                                bias_ref[off, hp])
            o_ref[pl.ds(q0, GRID_W), LANES * hp:LANES * (hp + 1)] = o.astype(BF16)
        return carry

    lax.fori_loop(0, NA_GROUP, body, 0)


def _natten(q, k, v, bias):
    b, t, w = q.shape
    n_rows = t // GRID_W
    n_groups = n_rows // NA_GROUP
    blk = lambda f: pl.BlockSpec((None, NA_TOK, w), lambda bi, g: (bi, f(g), 0))
    cur = blk(lambda g: g)
    prev = blk(lambda g: jnp.maximum(g - 1, 0))
    nxt = blk(lambda g: jnp.minimum(g + 1, n_groups - 1))
    return pl.pallas_call(
        functools.partial(_natten_kernel, n_rows=n_rows),
        grid=(b, n_groups),
        in_specs=[cur, prev, cur, nxt, prev, cur, nxt, _const_spec(bias.shape)],
        out_specs=cur,
        out_shape=jax.ShapeDtypeStruct((b, t, w), BF16),
        scratch_shapes=[pltpu.VMEM((3 * NA_TOK, w), BF16), pltpu.VMEM((3 * NA_TOK, 2 * w), BF16)],
        compiler_params=_cparams(2),
        name="natten",
    )(q, k, k, k, v, v, v, bias)


def _natten_bias(rpb):
    off = np.arange(NA_ROWS)[:, None]
    drow = np.arange(NA_ROWS)[None, :] - off + NA_ROWS - 1
    c = np.arange(GRID_W)[:, None]
    kc = np.arange(GRID_W)[None, :]
    c0 = np.clip(c - NA_COLS // 2, 0, GRID_W - NA_COLS)
    valid = (kc >= c0) & (kc < c0 + NA_COLS)
    dcol = np.clip(kc - c + NA_COLS - 1, 0, 2 * NA_COLS - 2)
    tab = rpb.astype(F32)[:, drow][:, :, :, dcol]
    tab = jnp.where(valid[None, None, None], tab, NEG)
    tab = tab.transpose(1, 0, 3, 2, 4)
    return tab.reshape(NA_ROWS, N_HEADS_B // 2, 2 * GRID_W, NA_KEYS)


FIN_TM = 512
FF_CHUNK = 1024


def _final_kernel(x_ref, o1_ref, o4_ref, o16_ref, l1_ref, l4_ref, l16_ref, ob_ref, qm_ref,
                  km_ref, vm_ref, ga_ref, gb_ref, gm_ref, wo_ref, gf_ref, w1_ref, w2_ref, y_ref):
    l1, l4, l16 = l1_ref[...], l4_ref[...], l16_ref[...]
    mx = jnp.maximum(jnp.maximum(l1, l4), l16)
    e1, e4, e16 = jnp.exp(l1 - mx), jnp.exp(l4 - mx), jnp.exp(l16 - mx)
    oa = (e1 * o1_ref[...].astype(F32) + e4 * o4_ref[...].astype(F32)
          + e16 * o16_ref[...].astype(F32)) / (e1 + e4 + e16)
    om = jnp.concatenate(
        [_attend_pair(qm_ref[:, LANES * hp:LANES * (hp + 1)], km_ref[:, LANES * hp:LANES * (hp + 1)],
                      vm_ref[:, 2 * LANES * hp:2 * LANES * (hp + 1)], None)[0]
         for hp in range(WIDTH_M // LANES)], axis=1)
    mixed = jnp.concatenate([_rms(oa, ga_ref[...]), _rms(ob_ref[...].astype(F32), gb_ref[...]),
                             _rms(om, gm_ref[...])], axis=1).astype(BF16)
    x1 = x_ref[...] + jnp.dot(mixed, wo_ref[...], preferred_element_type=F32)
    hf = _rms(x1, gf_ref[...]).astype(BF16)
    acc = x1
    for c in range(0, D_FF, FF_CHUNK):
        h = jnp.dot(hf, w1_ref[:, c:c + FF_CHUNK], preferred_element_type=F32)
        h = jnp.square(jnp.maximum(h, 0.0)).astype(BF16)
        acc = acc + jnp.dot(h, w2_ref[c:c + FF_CHUNK, :], preferred_element_type=F32)
    y_ref[...] = acc


def _final(x, o1, o4, o16, l1, l4, l16, ob, qm, km, vm, ga, gb, gm, wo, gf, w1, w2):
    b, t, _ = x.shape
    tm = FIN_TM
    tok = lambda width: pl.BlockSpec((None, tm, width), lambda bi, i: (bi, i, 0))
    per_b = lambda rows, width: pl.BlockSpec((None, rows, width), lambda bi, i: (bi, 0, 0))
    return pl.pallas_call(
        _final_kernel,
        grid=(b, t // tm),
        in_specs=[tok(D_MODEL), tok(WIDTH_A), tok(WIDTH_A), tok(WIDTH_A),
                  tok(WIDTH_A), tok(WIDTH_A), tok(WIDTH_A), tok(WIDTH_B), tok(WIDTH_M),
                  per_b(N_MEM, WIDTH_M), per_b(N_MEM, 2 * WIDTH_M),
                  _const_spec((1, WIDTH_A)), _const_spec((1, WIDTH_B)), _const_spec((1, WIDTH_M)),
                  _const_spec((D_MODEL, D_MODEL)), _const_spec((1, D_MODEL)),
                  _const_spec((D_MODEL, D_FF)), _const_spec((D_FF, D_MODEL))],
        out_specs=tok(D_MODEL),
        out_shape=jax.ShapeDtypeStruct((b, t, D_MODEL), F32),
        compiler_params=_cparams(2),
        name="final",
    )(x, o1, o4, o16, l1, l4, l16, ob, qm, km, vm, ga, gb, gm, wo, gf, w1, w2)


def _rope_tables(t):
    half = ROPE_DIMS // 2
    inv = ROPE_THETA ** (-(np.arange(half, dtype=np.float64) * 2.0 / ROPE_DIMS))
    ang = np.arange(t, dtype=np.float64)[:, None] * inv[None, :]
    cs = jnp.asarray(np.concatenate([np.cos(ang), np.sin(ang)], axis=1), dtype=F32)
    cos, sin = cs[:, :half], cs[:, half:]
    rest = HEAD_DIM - ROPE_DIMS
    cos_h = jnp.concatenate([cos, cos, jnp.ones((t, rest), F32)], axis=1)
    sin_h = jnp.concatenate([-sin, sin, jnp.zeros((t, rest), F32)], axis=1)
    return jnp.tile(cos_h, (1, LANES // HEAD_DIM)), jnp.tile(sin_h, (1, LANES // HEAD_DIM))


def _layer(x, mem, p, tables):
    cos_t, sin_t = tables[x.shape[1]]
    qa, ka, va, qb, kb, vb, qm = _proj(x, p["g_attn"], p["w_in"], p["bd"], p["gqa"], p["gka"],
                                       p["gqb"], p["gkb"], p["gqm"], cos_t, sin_t)
    km, vm = _memkv(mem, p["g_mem"], p["w_kv"], p["bd"], p["gkm"])
    pats = [_dilated(qa, ka, va, p["band"], d) for d in DILATIONS]
    ob = _natten(qb, kb, vb, p["na_bias"])
    return _final(x, pats[0][0], pats[1][0], pats[2][0], pats[0][1], pats[1][1], pats[2][1],
                  ob, qm, km, vm, p["ga"], p["gb"], p["gm"], p["w_out"], p["g_ffn"], p["w1"], p["w2"])


def kernel(x_prompt, x_sample, mem_prompt, mem_sample, norm_attn, w_in, q_norm_a, k_norm_a, q_norm_b, k_norm_b, rpb_b, norm_mem, w_mem_kv, q_norm_m, k_norm_m, out_norm_a, out_norm_b, out_norm_m, w_out, norm_ffn, w_ff1, w_ff2):
    depth = w_in.shape[0]
    scale = HEAD_DIM ** -0.5
    row = lambda v: v.astype(F32)[None, :]
    heads = lambda v, n: jnp.tile(v.astype(F32), n)[None, :]
    idx = np.arange(2 * LANES)
    bd = jnp.asarray(idx[:, None] // HEAD_DIM == idx[None, :] // HEAD_DIM, dtype=BF16)
    band = _band_bias()
    tables = {t: _rope_tables(t) for t in {x_prompt.shape[1], x_sample.shape[1]}}
    y_prompt, y_sample = x_prompt, x_sample
    for i in range(depth):
        p = dict(
            g_attn=row(norm_attn[i]), w_in=w_in[i].astype(BF16), bd=bd, band=band,
            gqa=heads(q_norm_a[i], N_HEADS_A) * scale, gka=heads(k_norm_a[i], N_HEADS_A),
            gqb=heads(q_norm_b[i], N_HEADS_B) * scale, gkb=heads(k_norm_b[i], N_HEADS_B),
            gqm=heads(q_norm_m[i], N_HEADS_M) * scale, gkm=heads(k_norm_m[i], N_HEADS_M),
            na_bias=_natten_bias(rpb_b[i]),
            g_mem=row(norm_mem[i]), w_kv=w_mem_kv[i].astype(BF16),
            ga=row(out_norm_a[i]), gb=row(out_norm_b[i]), gm=row(out_norm_m[i]),
            w_out=w_out[i].astype(BF16), g_ffn=row(norm_ffn[i]),
            w1=w_ff1[i].astype(BF16), w2=w_ff2[i].astype(BF16))
        y_prompt = _layer(y_prompt, mem_prompt, p, tables)
        y_sample = _layer(y_sample, mem_sample, p, tables)
    return (y_prompt, y_sample)
```

```python
import functools

import numpy as np
import jax
import jax.numpy as jnp
from jax import lax
from jax.experimental import pallas as pl
from jax.experimental.pallas import tpu as pltpu

D_MODEL = 1024
HEAD_DIM = 64
N_HEADS_A = 6
N_HEADS_B = 6
N_HEADS_M = 4
WIDTH_A = N_HEADS_A * HEAD_DIM
WIDTH_B = N_HEADS_B * HEAD_DIM
WIDTH_M = N_HEADS_M * HEAD_DIM
IN_WIDTH = 3 * WIDTH_A + 3 * WIDTH_B + WIDTH_M
DILATIONS = (1, 4, 16)
HALF = 64
GRID_W = 64
NA_ROWS = 8
NA_COLS = 16
N_MEM = 256
D_FF = 4 * D_MODEL
ROPE_THETA = 500000.0
ROPE_DIMS = HEAD_DIM // 4
EPS = 1e-6
NEG = -1e30

LANES = 128
PAIR = 2 * HEAD_DIM
VMEM_LIMIT = 56 * 1024 * 1024

BF16 = jnp.bfloat16
F32 = jnp.float32


def _cparams(n_axes):
    return pltpu.CompilerParams(dimension_semantics=("parallel",) * n_axes,
                                vmem_limit_bytes=VMEM_LIMIT)


def _const_spec(shape):
    nd = len(shape)
    return pl.BlockSpec(shape, lambda *_: (0,) * nd, pipeline_mode=pl.Buffered(1))


def _rms(x, gain):
    return x * lax.rsqrt(jnp.mean(x * x, axis=-1, keepdims=True) + EPS) * gain


def _head_rms(z, bd, gain):
    z2 = z * z
    hi = z2.astype(BF16)
    lo = (z2 - hi.astype(F32)).astype(BF16)
    ss = (jnp.dot(hi, bd, preferred_element_type=F32) + jnp.dot(lo, bd, preferred_element_type=F32))
    return z * lax.rsqrt(ss * (1.0 / HEAD_DIM) + EPS) * gain


def _attend_pair(qp, kp, vaug, bias):
    m_rows = qp.shape[0]
    lane = lax.broadcasted_iota(jnp.int32, qp.shape, 1)
    zero = jnp.zeros_like(qp)
    lhs = jnp.concatenate([jnp.where(lane < HEAD_DIM, qp, zero),
                           jnp.where(lane >= HEAD_DIM, qp, zero)], axis=0)
    s = lax.dot_general(lhs, kp, (((1,), (1,)), ((), ())), preferred_element_type=F32)
    if bias is not None:
        s = s + bias
    mx = jnp.max(s, axis=-1, keepdims=True)
    p = jnp.exp(s - mx).astype(BF16)
    ov = jnp.dot(p, vaug, preferred_element_type=F32)
    first = lax.broadcasted_iota(jnp.int32, (m_rows, LANES), 1) < HEAD_DIM
    o = jnp.where(first, ov[:m_rows, :LANES], ov[m_rows:, :LANES])
    l = jnp.where(first, ov[:m_rows, LANES:], ov[m_rows:, LANES:])
    mm = jnp.where(first, mx[:m_rows], mx[m_rows:])
    return o / l, mm + jnp.log(l)


def _fill_vaug(vaug_ref, row0, v):
    rows = v.shape[0]
    ones = jnp.ones((rows, LANES), BF16)
    for hp in range(v.shape[1] // LANES):
        vaug_ref[row0:row0 + rows, 2 * LANES * hp:2 * LANES * hp + LANES] = v[:, LANES * hp:LANES * (hp + 1)]
        vaug_ref[row0:row0 + rows, 2 * LANES * hp + LANES:2 * LANES * (hp + 1)] = ones


PROJ_TM = 512


def _proj_kernel(x_ref, g_ref, w_ref, bd_ref, gqa_ref, gka_ref, gqb_ref, gkb_ref, gqm_ref,
                 cos_ref, sin_ref, *refs):
    n_dil = len(DILATIONS)
    qa_refs, ka_refs, va_refs = refs[:n_dil], refs[n_dil:2 * n_dil], refs[2 * n_dil:3 * n_dil]
    qb_ref, kb_ref, vb_ref, qm_ref, stage = refs[3 * n_dil:]
    tm = x_ref.shape[0]
    x = x_ref[...]
    xn = _rms(x, g_ref[...]).astype(BF16)
    proj = jnp.dot(xn, w_ref[...], preferred_element_type=F32)
    bd = bd_ref[...]

    def normed(c0, width, gain_ref):
        parts = []
        for off in range(0, width, 2 * LANES):
            w = min(2 * LANES, width - off)
            parts.append(_head_rms(proj[:, c0 + off:c0 + off + w], bd[:w, :w], gain_ref[:, off:off + w]))
        return jnp.concatenate(parts, axis=1) if len(parts) > 1 else parts[0]

    def rope(y):
        cos = cos_ref[...]
        sin = sin_ref[...]
        lane = lax.broadcasted_iota(jnp.int32, cos.shape, 1) % HEAD_DIM
        low = lane < ROPE_DIMS // 2
        parts = []
        for c in range(0, y.shape[1], LANES):
            yc = y[:, c:c + LANES]
            partner = jnp.where(low, pltpu.roll(yc, LANES - ROPE_DIMS // 2, axis=1),
                                pltpu.roll(yc, ROPE_DIMS // 2, axis=1))
            parts.append(yc * cos + partner * sin)
        return jnp.concatenate(parts, axis=1)

    def emit_classes(y, out_refs):
        for c in range(WIDTH_A // LANES):
            stage[c] = y[:, c * LANES:(c + 1) * LANES]
        for dil, ref in zip(DILATIONS, out_refs):
            if dil == 1:
                ref[...] = y.astype(BF16)
                continue
            for r in range(dil):
                for c in range(WIDTH_A // LANES):
                    col = r * WIDTH_A + c * LANES
                    ref[:, col:col + LANES] = stage[c, pl.ds(r, tm // dil, stride=dil), :].astype(BF16)

    o = 0
    emit_classes(rope(normed(o, WIDTH_A, gqa_ref)), qa_refs); o += WIDTH_A
    emit_classes(rope(normed(o, WIDTH_A, gka_ref)), ka_refs); o += WIDTH_A
    emit_classes(proj[:, o:o + WIDTH_A], va_refs); o += WIDTH_A
    qb_ref[...] = normed(o, WIDTH_B, gqb_ref).astype(BF16); o += WIDTH_B
    kb_ref[...] = normed(o, WIDTH_B, gkb_ref).astype(BF16); o += WIDTH_B
    vb_ref[...] = proj[:, o:o + WIDTH_B].astype(BF16); o += WIDTH_B
    qm_ref[...] = normed(o, WIDTH_M, gqm_ref).astype(BF16)


def _proj(x, g, w_in, bd, gqa, gka, gqb, gkb, gqm, cos_t, sin_t):
    b, t, _ = x.shape
    tm = PROJ_TM
    tok = lambda width: pl.BlockSpec((None, tm, width), lambda bi, i: (bi, i, 0))
    cls = lambda dil: pl.BlockSpec((None, tm // dil, dil * WIDTH_A), lambda bi, i: (bi, i, 0))
    cls_shape = lambda dil: jax.ShapeDtypeStruct((b, t // dil, dil * WIDTH_A), BF16)
    rest_widths = (WIDTH_B, WIDTH_B, WIDTH_B, WIDTH_M)
    n = len(DILATIONS)
    outs = pl.pallas_call(
        _proj_kernel,
        grid=(b, t // tm),
        in_specs=[tok(D_MODEL), _const_spec((1, D_MODEL)), _const_spec((D_MODEL, IN_WIDTH)),
                  _const_spec((2 * LANES, 2 * LANES)),
                  _const_spec((1, WIDTH_A)), _const_spec((1, WIDTH_A)),
                  _const_spec((1, WIDTH_B)), _const_spec((1, WIDTH_B)), _const_spec((1, WIDTH_M)),
                  pl.BlockSpec((tm, LANES), lambda bi, i: (i, 0)),
                  pl.BlockSpec((tm, LANES), lambda bi, i: (i, 0))],
        out_specs=[cls(d) for d in DILATIONS] * 3 + [tok(w) for w in rest_widths],
        out_shape=[cls_shape(d) for d in DILATIONS] * 3
                  + [jax.ShapeDtypeStruct((b, t, w), BF16) for w in rest_widths],
        scratch_shapes=[pltpu.VMEM((WIDTH_A // LANES, tm, LANES), F32)],
        compiler_params=_cparams(2),
        name="proj",
    )(x, g, w_in, bd, gqa, gka, gqb, gkb, gqm, cos_t, sin_t)
    return outs[:n], outs[n:2 * n], outs[2 * n:3 * n], outs[3 * n:]


def _memkv_kernel(mem_ref, g_ref, w_ref, bd_ref, gk_ref, km_ref, vaug_ref):
    mn = _rms(mem_ref[...], g_ref[...]).astype(BF16)
    kv = jnp.dot(mn, w_ref[...], preferred_element_type=F32)
    km_ref[...] = _head_rms(kv[:, :WIDTH_M], bd_ref[...], gk_ref[...]).astype(BF16)
    _fill_vaug(vaug_ref, 0, kv[:, WIDTH_M:].astype(BF16))


def _memkv(mem, g, w_kv, bd, gk):
    b = mem.shape[0]
    return pl.pallas_call(
        _memkv_kernel,
        grid=(b,),
        in_specs=[pl.BlockSpec((None, N_MEM, D_MODEL), lambda bi: (bi, 0, 0)),
                  _const_spec((1, D_MODEL)), _const_spec((D_MODEL, 2 * WIDTH_M)),
                  _const_spec((2 * LANES, 2 * LANES)), _const_spec((1, WIDTH_M))],
        out_specs=[pl.BlockSpec((None, N_MEM, WIDTH_M), lambda bi: (bi, 0, 0)),
                   pl.BlockSpec((None, N_MEM, 2 * WIDTH_M), lambda bi: (bi, 0, 0))],
        out_shape=[jax.ShapeDtypeStruct((b, N_MEM, WIDTH_M), BF16),
                   jax.ShapeDtypeStruct((b, N_MEM, 2 * WIDTH_M), BF16)],
        compiler_params=_cparams(1),
        name="memkv",
    )(mem, g, w_kv, bd, gk)


DIL_QB = 512
DIL_SUB = 2 * HALF


def _dilated_kernel(q_ref, kp_ref, kc_ref, kn_ref, vp_ref, vc_ref, vn_ref, bias_ref,
                    o_ref, lse_ref, kbuf, vaug, *, n_sub_total):
    qb = q_ref.shape[0]
    kbuf[0:HALF] = kp_ref[...]
    kbuf[HALF:HALF + qb] = kc_ref[...]
    kbuf[HALF + qb:2 * HALF + qb] = kn_ref[...]
    _fill_vaug(vaug, 0, vp_ref[...])
    _fill_vaug(vaug, HALF, vc_ref[...])
    _fill_vaug(vaug, HALF + qb, vn_ref[...])
    n_sub = qb // DIL_SUB
    first_sub = pl.program_id(2) * n_sub

    for j in range(n_sub):
        gs = first_sub + j
        variant = jnp.where(gs == 0, 0, jnp.where(gs == n_sub_total - 1, 2, 1))
        r0 = j * DIL_SUB
        for hp in range(WIDTH_A // LANES):
            o, lse = _attend_pair(q_ref[r0:r0 + DIL_SUB, LANES * hp:LANES * (hp + 1)],
                                  kbuf[r0:r0 + 2 * DIL_SUB, LANES * hp:LANES * (hp + 1)],
                                  vaug[r0:r0 + 2 * DIL_SUB, 2 * LANES * hp:2 * LANES * (hp + 1)],
                                  bias_ref[variant])
            o_ref[r0:r0 + DIL_SUB, LANES * hp:LANES * (hp + 1)] = o.astype(BF16)
            lse_ref[r0:r0 + DIL_SUB, LANES * hp:LANES * (hp + 1)] = lse


def _dilated(q, k, v, bias, dil):
    b, ln, w = q.shape[0], q.shape[1], q.shape[2] // dil
    qb = min(DIL_QB, ln)
    hb = qb // HALF
    n_halo = ln // HALF
    main = pl.BlockSpec((None, qb, w), lambda bi, r, i: (bi, i, r))
    prev = pl.BlockSpec((None, HALF, w), lambda bi, r, i: (bi, jnp.maximum(i * hb - 1, 0), r))
    nxt = pl.BlockSpec((None, HALF, w), lambda bi, r, i: (bi, jnp.minimum((i + 1) * hb, n_halo - 1), r))
    o, lse = pl.pallas_call(
        functools.partial(_dilated_kernel, n_sub_total=ln // DIL_SUB),
        grid=(b, dil, ln // qb),
        in_specs=[main, prev, main, nxt, prev, main, nxt, _const_spec(bias.shape)],
        out_specs=[main, main],
        out_shape=[jax.ShapeDtypeStruct((b, ln, dil * w), BF16),
                   jax.ShapeDtypeStruct((b, ln, dil * w), F32)],
        scratch_shapes=[pltpu.VMEM((qb + 2 * HALF, w), BF16),
                        pltpu.VMEM((qb + 2 * HALF, 2 * w), BF16)],
        compiler_params=_cparams(3),
        name=f"dilated{dil}",
    )(q, k, k, k, v, v, v, bias)
    return o, lse


def _band_bias():
    row = np.arange(DIL_SUB)[:, None]
    col = np.arange(2 * DIL_SUB)[None, :]
    band = (col - row >= 0) & (col - row <= 2 * HALF)
    variants = [band & (col >= HALF), band, band & (col < 2 * DIL_SUB - HALF)]
    tab = np.stack([np.where(np.concatenate([m, m], axis=0), 0.0, NEG) for m in variants])
    return jnp.asarray(tab, dtype=F32)


NA_GROUP = 8
NA_TOK = NA_GROUP * GRID_W
NA_KEYS = NA_ROWS * GRID_W
NA_UNROLL = 2


def _natten_kernel(q_ref, kp_ref, kc_ref, kn_ref, vp_ref, vc_ref, vn_ref, bias_ref,
                   o_ref, kbuf, vaug, *, n_rows):
    kbuf[0:NA_TOK] = kp_ref[...]
    kbuf[NA_TOK:2 * NA_TOK] = kc_ref[...]
    kbuf[2 * NA_TOK:3 * NA_TOK] = kn_ref[...]
    _fill_vaug(vaug, 0, vp_ref[...])
    _fill_vaug(vaug, NA_TOK, vc_ref[...])
    _fill_vaug(vaug, 2 * NA_TOK, vn_ref[...])
    g = pl.program_id(1)

    def body(j, carry):
        r = g * NA_GROUP + j
        r0 = jnp.clip(r - NA_ROWS // 2, 0, n_rows - NA_ROWS)
        off = r - r0
        start = pl.multiple_of((r0 - (g - 1) * NA_GROUP) * GRID_W, GRID_W)
        q0 = pl.multiple_of(j * GRID_W, GRID_W)
        for hp in range(WIDTH_B // LANES):
            o, _ = _attend_pair(q_ref[pl.ds(q0, GRID_W), LANES * hp:LANES * (hp + 1)],
                                kbuf[pl.ds(start, NA_KEYS), LANES * hp:LANES * (hp + 1)],
                                vaug[pl.ds(start, NA_KEYS), 2 * LANES * hp:2 * LANES * (hp + 1)],
                                bias_ref[off, hp])
            o_ref[pl.ds(q0, GRID_W), LANES * hp:LANES * (hp + 1)] = o.astype(BF16)
        return carry

    lax.fori_loop(0, NA_GROUP, body, 0, unroll=NA_UNROLL)


def _natten(q, k, v, bias):
    b, t, w = q.shape
    n_rows = t // GRID_W
    n_groups = n_rows // NA_GROUP
    blk = lambda f: pl.BlockSpec((None, NA_TOK, w), lambda bi, g: (bi, f(g), 0))
    cur = blk(lambda g: g)
    prev = blk(lambda g: jnp.maximum(g - 1, 0))
    nxt = blk(lambda g: jnp.minimum(g + 1, n_groups - 1))
    return pl.pallas_call(
        functools.partial(_natten_kernel, n_rows=n_rows),
        grid=(b, n_groups),
        in_specs=[cur, prev, cur, nxt, prev, cur, nxt, _const_spec(bias.shape)],
        out_specs=cur,
        out_shape=jax.ShapeDtypeStruct((b, t, w), BF16),
        scratch_shapes=[pltpu.VMEM((3 * NA_TOK, w), BF16), pltpu.VMEM((3 * NA_TOK, 2 * w), BF16)],
        compiler_params=_cparams(2),
        name="natten",
    )(q, k, k, k, v, v, v, bias)


def _natten_bias(rpb):
    off = np.arange(NA_ROWS)[:, None]
    drow = np.arange(NA_ROWS)[None, :] - off + NA_ROWS - 1
    c = np.arange(GRID_W)[:, None]
    kc = np.arange(GRID_W)[None, :]
    c0 = np.clip(c - NA_COLS // 2, 0, GRID_W - NA_COLS)
    valid = (kc >= c0) & (kc < c0 + NA_COLS)
    dcol = np.clip(kc - c + NA_COLS - 1, 0, 2 * NA_COLS - 2)
    tab = rpb.astype(F32)[:, drow][:, :, :, dcol]
    tab = jnp.where(valid[None, None, None], tab, NEG)
    tab = tab.transpose(1, 0, 3, 2, 4)
    return tab.reshape(NA_ROWS, N_HEADS_B // 2, 2 * GRID_W, NA_KEYS)


FIN_TM = 512
FF_CHUNK = 1024


def _final_kernel(x_ref, o1_ref, o4_ref, o16_ref, l1_ref, l4_ref, l16_ref, ob_ref, qm_ref,
                  km_ref, vm_ref, ga_ref, gb_ref, gm_ref, wo_ref, gf_ref, w1_ref, w2_ref, y_ref,
                  *stages):
    tm = x_ref.shape[0]

    def token_order(ref, dil, stage):
        if dil == 1:
            return ref[...].astype(F32)
        n_tiles = WIDTH_A // LANES
        for r in range(dil):
            for c in range(n_tiles):
                col = r * WIDTH_A + c * LANES
                stage[c, pl.ds(r, tm // dil, stride=dil), :] = ref[:, col:col + LANES].astype(F32)
        return jnp.concatenate([stage[c] for c in range(n_tiles)], axis=1)

    o1, l1 = token_order(o1_ref, DILATIONS[0], None), token_order(l1_ref, DILATIONS[0], None)
    o4, l4 = token_order(o4_ref, DILATIONS[1], stages[0]), token_order(l4_ref, DILATIONS[1], stages[1])
    o16, l16 = token_order(o16_ref, DILATIONS[2], stages[2]), token_order(l16_ref, DILATIONS[2], stages[3])
    mx = jnp.maximum(jnp.maximum(l1, l4), l16)
    e1, e4, e16 = jnp.exp(l1 - mx), jnp.exp(l4 - mx), jnp.exp(l16 - mx)
    oa = (e1 * o1 + e4 * o4 + e16 * o16) / (e1 + e4 + e16)
    om = jnp.concatenate(
        [_attend_pair(qm_ref[:, LANES * hp:LANES * (hp + 1)], km_ref[:, LANES * hp:LANES * (hp + 1)],
                      vm_ref[:, 2 * LANES * hp:2 * LANES * (hp + 1)], None)[0]
         for hp in range(WIDTH_M // LANES)], axis=1)
    mixed = jnp.concatenate([_rms(oa, ga_ref[...]), _rms(ob_ref[...].astype(F32), gb_ref[...]),
                             _rms(om, gm_ref[...])], axis=1).astype(BF16)
    x1 = x_ref[...] + jnp.dot(mixed, wo_ref[...], preferred_element_type=F32)
    hf = _rms(x1, gf_ref[...]).astype(BF16)
    acc = x1
    for c in range(0, D_FF, FF_CHUNK):
        h = jnp.dot(hf, w1_ref[:, c:c + FF_CHUNK], preferred_element_type=F32)
        h = jnp.square(jnp.maximum(h, 0.0)).astype(BF16)
        acc = acc + jnp.dot(h, w2_ref[c:c + FF_CHUNK, :], preferred_element_type=F32)
    y_ref[...] = acc


def _final(x, o1, o4, o16, l1, l4, l16, ob, qm, km, vm, ga, gb, gm, wo, gf, w1, w2):
    b, t, _ = x.shape
    tm = FIN_TM
    tok = lambda width: pl.BlockSpec((None, tm, width), lambda bi, i: (bi, i, 0))
    per_b = lambda rows, width: pl.BlockSpec((None, rows, width), lambda bi, i: (bi, 0, 0))
    cls = [pl.BlockSpec((None, tm // d, d * WIDTH_A), lambda bi, i: (bi, i, 0)) for d in DILATIONS]
    return pl.pallas_call(
        _final_kernel,
        grid=(b, t // tm),
        in_specs=[tok(D_MODEL), *cls, *cls, tok(WIDTH_B), tok(WIDTH_M),
                  per_b(N_MEM, WIDTH_M), per_b(N_MEM, 2 * WIDTH_M),
                  _const_spec((1, WIDTH_A)), _const_spec((1, WIDTH_B)), _const_spec((1, WIDTH_M)),
                  _const_spec((D_MODEL, D_MODEL)), _const_spec((1, D_MODEL)),
                  _const_spec((D_MODEL, D_FF)), _const_spec((D_FF, D_MODEL))],
        out_specs=tok(D_MODEL),
        out_shape=jax.ShapeDtypeStruct((b, t, D_MODEL), F32),
        scratch_shapes=[pltpu.VMEM((WIDTH_A // LANES, tm, LANES), F32)] * 4,
        compiler_params=_cparams(2),
        name="final",
    )(x, o1, o4, o16, l1, l4, l16, ob, qm, km, vm, ga, gb, gm, wo, gf, w1, w2)


def _rope_tables(t):
    half = ROPE_DIMS // 2
    inv = ROPE_THETA ** (-(np.arange(half, dtype=np.float64) * 2.0 / ROPE_DIMS))
    ang = np.arange(t, dtype=np.float64)[:, None] * inv[None, :]
    cs = jnp.asarray(np.concatenate([np.cos(ang), np.sin(ang)], axis=1), dtype=F32)
    cos, sin = cs[:, :half], cs[:, half:]
    rest = HEAD_DIM - ROPE_DIMS
    cos_h = jnp.concatenate([cos, cos, jnp.ones((t, rest), F32)], axis=1)
    sin_h = jnp.concatenate([-sin, sin, jnp.zeros((t, rest), F32)], axis=1)
    return jnp.tile(cos_h, (1, LANES // HEAD_DIM)), jnp.tile(sin_h, (1, LANES // HEAD_DIM))


def _layer(x, mem, p, tables):
    cos_t, sin_t = tables[x.shape[1]]
    qa, ka, va, (qb, kb, vb, qm) = _proj(x, p["g_attn"], p["w_in"], p["bd"], p["gqa"], p["gka"],
                                         p["gqb"], p["gkb"], p["gqm"], cos_t, sin_t)
    km, vm = _memkv(mem, p["g_mem"], p["w_kv"], p["bd"], p["gkm"])
    pats = [_dilated(qa[n], ka[n], va[n], p["band"], d) for n, d in enumerate(DILATIONS)]
    ob = _natten(qb, kb, vb, p["na_bias"])
    return _final(x, pats[0][0], pats[1][0], pats[2][0], pats[0][1], pats[1][1], pats[2][1],
                  ob, qm, km, vm, p["ga"], p["gb"], p["gm"], p["w_out"], p["g_ffn"], p["w1"], p["w2"])


def kernel(x_prompt, x_sample, mem_prompt, mem_sample, norm_attn, w_in, q_norm_a, k_norm_a, q_norm_b, k_norm_b, rpb_b, norm_mem, w_mem_kv, q_norm_m, k_norm_m, out_norm_a, out_norm_b, out_norm_m, w_out, norm_ffn, w_ff1, w_ff2):
    depth = w_in.shape[0]
    scale = HEAD_DIM ** -0.5
    row = lambda v: v.astype(F32)[None, :]
    heads = lambda v, n: jnp.tile(v.astype(F32), n)[None, :]
    idx = np.arange(2 * LANES)
    bd = jnp.asarray(idx[:, None] // HEAD_DIM == idx[None, :] // HEAD_DIM, dtype=BF16)
    band = _band_bias()
    tables = {t: _rope_tables(t) for t in {x_prompt.shape[1], x_sample.shape[1]}}
    y_prompt, y_sample = x_prompt, x_sample
    for i in range(depth):
        p = dict(
            g_attn=row(norm_attn[i]), w_in=w_in[i].astype(BF16), bd=bd, band=band,
            gqa=heads(q_norm_a[i], N_HEADS_A) * scale, gka=heads(k_norm_a[i], N_HEADS_A),
            gqb=heads(q_norm_b[i], N_HEADS_B) * scale, gkb=heads(k_norm_b[i], N_HEADS_B),
            gqm=heads(q_norm_m[i], N_HEADS_M) * scale, gkm=heads(k_norm_m[i], N_HEADS_M),
            na_bias=_natten_bias(rpb_b[i]),
            g_mem=row(norm_mem[i]), w_kv=w_mem_kv[i].astype(BF16),
            ga=row(out_norm_a[i]), gb=row(out_norm_b[i]), gm=row(out_norm_m[i]),
            w_out=w_out[i].astype(BF16), g_ffn=row(norm_ffn[i]),
            w1=w_ff1[i].astype(BF16), w2=w_ff2[i].astype(BF16))
        y_prompt = _layer(y_prompt, mem_prompt, p, tables)
        y_sample = _layer(y_sample, mem_sample, p, tables)
    return (y_prompt, y_sample)
```

```python
import functools

import numpy as np
import jax
import jax.numpy as jnp
from jax import lax
from jax.experimental import pallas as pl
from jax.experimental.pallas import tpu as pltpu

D_MODEL = 1024
HEAD_DIM = 64
N_HEADS_A = 6
N_HEADS_B = 6
N_HEADS_M = 4
WIDTH_A = N_HEADS_A * HEAD_DIM
WIDTH_B = N_HEADS_B * HEAD_DIM
WIDTH_M = N_HEADS_M * HEAD_DIM
IN_WIDTH = 3 * WIDTH_A + 3 * WIDTH_B + WIDTH_M
DILATIONS = (1, 4, 16)
DIL_STEP = 4
HALF = 64
GRID_W = 64
NA_ROWS = 8
NA_COLS = 16
N_MEM = 256
D_FF = 4 * D_MODEL
ROPE_THETA = 500000.0
ROPE_DIMS = HEAD_DIM // 4
EPS = 1e-6
NEG = -1e30

LANES = 128
PAIR = 2 * HEAD_DIM
VMEM_LIMIT = 56 * 1024 * 1024

BF16 = jnp.bfloat16
F32 = jnp.float32


def _cparams(n_axes):
    return pltpu.CompilerParams(dimension_semantics=("parallel",) * n_axes,
                                vmem_limit_bytes=VMEM_LIMIT)


def _const_spec(shape):
    nd = len(shape)
    return pl.BlockSpec(shape, lambda *_: (0,) * nd, pipeline_mode=pl.Buffered(1))


def _rms(x, gain):
    return x * lax.rsqrt(jnp.mean(x * x, axis=-1, keepdims=True) + EPS) * gain


def _head_rms(z, bd, gain):
    ss = jnp.dot((z * z).astype(BF16), bd, preferred_element_type=F32)
    return z * lax.rsqrt(ss * (1.0 / HEAD_DIM) + EPS) * gain


def _attend_pair(qp, kp, vaug, bias):
    m_rows = qp.shape[0]
    lane = lax.broadcasted_iota(jnp.int32, qp.shape, 1)
    zero = jnp.zeros_like(qp)
    lhs = jnp.concatenate([jnp.where(lane < HEAD_DIM, qp, zero),
                           jnp.where(lane >= HEAD_DIM, qp, zero)], axis=0)
    s = lax.dot_general(lhs, kp, (((1,), (1,)), ((), ())), preferred_element_type=F32)
    if bias is not None:
        s = s + bias
    mx = jnp.max(s, axis=-1, keepdims=True)
    p = jnp.exp(s - mx).astype(BF16)
    ov = jnp.dot(p, vaug, preferred_element_type=F32)
    first = lax.broadcasted_iota(jnp.int32, (m_rows, LANES), 1) < HEAD_DIM
    o = jnp.where(first, ov[:m_rows, :LANES], ov[m_rows:, :LANES])
    l = jnp.where(first, ov[:m_rows, LANES:], ov[m_rows:, LANES:])
    mm = jnp.where(first, mx[:m_rows], mx[m_rows:])
    return o / l, mm + jnp.log(l)


def _fill_vaug(vaug_ref, row0, v):
    rows = v.shape[0]
    ones = jnp.ones((rows, LANES), BF16)
    for hp in range(v.shape[1] // LANES):
        vaug_ref[row0:row0 + rows, 2 * LANES * hp:2 * LANES * hp + LANES] = v[:, LANES * hp:LANES * (hp + 1)]
        vaug_ref[row0:row0 + rows, 2 * LANES * hp + LANES:2 * LANES * (hp + 1)] = ones


PROJ_TM = 512


def _proj_kernel(x_ref, g_ref, w_ref, bd_ref, gqa_ref, gka_ref, gqb_ref, gkb_ref, gqm_ref,
                 cos_ref, sin_ref, *refs):
    n_dil = len(DILATIONS)
    qa_refs, ka_refs, va_refs = refs[:n_dil], refs[n_dil:2 * n_dil], refs[2 * n_dil:3 * n_dil]
    qb_ref, kb_ref, vb_ref, qm_ref, stage_a, stage_b = refs[3 * n_dil:]
    tm = x_ref.shape[0]
    x = x_ref[...]
    xn = _rms(x, g_ref[...]).astype(BF16)
    proj = jnp.dot(xn, w_ref[...], preferred_element_type=F32)
    bd = bd_ref[...]

    def normed(c0, width, gain_ref):
        parts = []
        for off in range(0, width, 2 * LANES):
            w = min(2 * LANES, width - off)
            parts.append(_head_rms(proj[:, c0 + off:c0 + off + w], bd[:w, :w], gain_ref[:, off:off + w]))
        return jnp.concatenate(parts, axis=1) if len(parts) > 1 else parts[0]

    def rope(y):
        cos = cos_ref[...]
        sin = sin_ref[...]
        lane = lax.broadcasted_iota(jnp.int32, cos.shape, 1) % HEAD_DIM
        low = lane < ROPE_DIMS // 2
        parts = []
        for c in range(0, y.shape[1], LANES):
            yc = y[:, c:c + LANES]
            partner = jnp.where(low, pltpu.roll(yc, LANES - ROPE_DIMS // 2, axis=1),
                                pltpu.roll(yc, ROPE_DIMS // 2, axis=1))
            parts.append(yc * cos + partner * sin)
        return jnp.concatenate(parts, axis=1)

    def emit_classes(y, out_refs):
        n_tiles = WIDTH_A // LANES
        rows4, rows16 = tm // DIL_STEP, tm // (DIL_STEP * DIL_STEP)
        out_refs[0][...] = y.astype(BF16)
        for c in range(n_tiles):
            stage_a[c] = y[:, c * LANES:(c + 1) * LANES]
        for r4 in range(DIL_STEP):
            for c in range(n_tiles):
                cls = stage_a[c, pl.ds(r4, rows4, stride=DIL_STEP), :]
                col = r4 * WIDTH_A + c * LANES
                out_refs[1][:, col:col + LANES] = cls.astype(BF16)
                stage_b[c, r4 * rows4:(r4 + 1) * rows4, :] = cls
        for r4 in range(DIL_STEP):
            for q in range(DIL_STEP):
                for c in range(n_tiles):
                    col = (r4 + DIL_STEP * q) * WIDTH_A + c * LANES
                    out_refs[2][:, col:col + LANES] = (
                        stage_b[c, pl.ds(r4 * rows4 + q, rows16, stride=DIL_STEP), :].astype(BF16))

    o = 0
    emit_classes(rope(normed(o, WIDTH_A, gqa_ref)), qa_refs); o += WIDTH_A
    emit_classes(rope(normed(o, WIDTH_A, gka_ref)), ka_refs); o += WIDTH_A
    emit_classes(proj[:, o:o + WIDTH_A], va_refs); o += WIDTH_A
    qb_ref[...] = normed(o, WIDTH_B, gqb_ref).astype(BF16); o += WIDTH_B
    kb_ref[...] = normed(o, WIDTH_B, gkb_ref).astype(BF16); o += WIDTH_B
    vb_ref[...] = proj[:, o:o + WIDTH_B].astype(BF16); o += WIDTH_B
    qm_ref[...] = normed(o, WIDTH_M, gqm_ref).astype(BF16)


def _proj(x, g, w_in, bd, gqa, gka, gqb, gkb, gqm, cos_t, sin_t):
    b, t, _ = x.shape
    tm = PROJ_TM
    tok = lambda width: pl.BlockSpec((None, tm, width), lambda bi, i: (bi, i, 0))
    cls = lambda dil: pl.BlockSpec((None, tm // dil, dil * WIDTH_A), lambda bi, i: (bi, i, 0))
    cls_shape = lambda dil: jax.ShapeDtypeStruct((b, t // dil, dil * WIDTH_A), BF16)
    rest_widths = (WIDTH_B, WIDTH_B, WIDTH_B, WIDTH_M)
    n = len(DILATIONS)
    outs = pl.pallas_call(
        _proj_kernel,
        grid=(b, t // tm),
        in_specs=[tok(D_MODEL), _const_spec((1, D_MODEL)), _const_spec((D_MODEL, IN_WIDTH)),
                  _const_spec((2 * LANES, 2 * LANES)),
                  _const_spec((1, WIDTH_A)), _const_spec((1, WIDTH_A)),
                  _const_spec((1, WIDTH_B)), _const_spec((1, WIDTH_B)), _const_spec((1, WIDTH_M)),
                  pl.BlockSpec((tm, LANES), lambda bi, i: (i, 0)),
                  pl.BlockSpec((tm, LANES), lambda bi, i: (i, 0))],
        out_specs=[cls(d) for d in DILATIONS] * 3 + [tok(w) for w in rest_widths],
        out_shape=[cls_shape(d) for d in DILATIONS] * 3
                  + [jax.ShapeDtypeStruct((b, t, w), BF16) for w in rest_widths],
        scratch_shapes=[pltpu.VMEM((WIDTH_A // LANES, tm, LANES), F32)] * 2,
        compiler_params=_cparams(2),
        name="proj",
    )(x, g, w_in, bd, gqa, gka, gqb, gkb, gqm, cos_t, sin_t)
    return outs[:n], outs[n:2 * n], outs[2 * n:3 * n], outs[3 * n:]


def _memkv_kernel(mem_ref, g_ref, w_ref, bd_ref, gk_ref, km_ref, vaug_ref):
    mn = _rms(mem_ref[...], g_ref[...]).astype(BF16)
    kv = jnp.dot(mn, w_ref[...], preferred_element_type=F32)
    km_ref[...] = _head_rms(kv[:, :WIDTH_M], bd_ref[...], gk_ref[...]).astype(BF16)
    _fill_vaug(vaug_ref, 0, kv[:, WIDTH_M:].astype(BF16))


def _memkv(mem, g, w_kv, bd, gk):
    b = mem.shape[0]
    return pl.pallas_call(
        _memkv_kernel,
        grid=(b,),
        in_specs=[pl.BlockSpec((None, N_MEM, D_MODEL), lambda bi: (bi, 0, 0)),
                  _const_spec((1, D_MODEL)), _const_spec((D_MODEL, 2 * WIDTH_M)),
                  _const_spec((2 * LANES, 2 * LANES)), _const_spec((1, WIDTH_M))],
        out_specs=[pl.BlockSpec((None, N_MEM, WIDTH_M), lambda bi: (bi, 0, 0)),
                   pl.BlockSpec((None, N_MEM, 2 * WIDTH_M), lambda bi: (bi, 0, 0))],
        out_shape=[jax.ShapeDtypeStruct((b, N_MEM, WIDTH_M), BF16),
                   jax.ShapeDtypeStruct((b, N_MEM, 2 * WIDTH_M), BF16)],
        compiler_params=_cparams(1),
        name="memkv",
    )(mem, g, w_kv, bd, gk)


DIL_QB = 2048
DIL_SUB = 2 * HALF


def _dilated_kernel(q_ref, kp_ref, kc_ref, kn_ref, vp_ref, vc_ref, vn_ref, bias_ref,
                    o_ref, lse_ref, kbuf, vaug, *, n_sub_total):
    qb = q_ref.shape[0]
    kbuf[0:HALF] = kp_ref[...]
    kbuf[HALF:HALF + qb] = kc_ref[...]
    kbuf[HALF + qb:2 * HALF + qb] = kn_ref[...]
    _fill_vaug(vaug, 0, vp_ref[...])
    _fill_vaug(vaug, HALF, vc_ref[...])
    _fill_vaug(vaug, HALF + qb, vn_ref[...])
    n_sub = qb // DIL_SUB
    first_sub = pl.program_id(2) * n_sub

    for j in range(n_sub):
        gs = first_sub + j
        variant = jnp.where(gs == 0, 0, jnp.where(gs == n_sub_total - 1, 2, 1))
        r0 = j * DIL_SUB
        for hp in range(WIDTH_A // LANES):
            o, lse = _attend_pair(q_ref[r0:r0 + DIL_SUB, LANES * hp:LANES * (hp + 1)],
                                  kbuf[r0:r0 + 2 * DIL_SUB, LANES * hp:LANES * (hp + 1)],
                                  vaug[r0:r0 + 2 * DIL_SUB, 2 * LANES * hp:2 * LANES * (hp + 1)],
                                  bias_ref[variant])
            o_ref[r0:r0 + DIL_SUB, LANES * hp:LANES * (hp + 1)] = o.astype(BF16)
            lse_ref[r0:r0 + DIL_SUB, LANES * hp:LANES * (hp + 1)] = lse


def _dilated(q, k, v, bias, dil):
    b, ln, w = q.shape[0], q.shape[1], q.shape[2] // dil
    qb = min(DIL_QB, ln)
    hb = qb // HALF
    n_halo = ln // HALF
    main = pl.BlockSpec((None, qb, w), lambda bi, r, i: (bi, i, r))
    prev = pl.BlockSpec((None, HALF, w), lambda bi, r, i: (bi, jnp.maximum(i * hb - 1, 0), r))
    nxt = pl.BlockSpec((None, HALF, w), lambda bi, r, i: (bi, jnp.minimum((i + 1) * hb, n_halo - 1), r))
    o, lse = pl.pallas_call(
        functools.partial(_dilated_kernel, n_sub_total=ln // DIL_SUB),
        grid=(b, dil, ln // qb),
        in_specs=[main, prev, main, nxt, prev, main, nxt, _const_spec(bias.shape)],
        out_specs=[main, main],
        out_shape=[jax.ShapeDtypeStruct((b, ln, dil * w), BF16),
                   jax.ShapeDtypeStruct((b, ln, dil * w), F32)],
        scratch_shapes=[pltpu.VMEM((qb + 2 * HALF, w), BF16),
                        pltpu.VMEM((qb + 2 * HALF, 2 * w), BF16)],
        compiler_params=_cparams(3),
        name=f"dilated{dil}",
    )(q, k, k, k, v, v, v, bias)
    return o, lse


def _band_bias():
    row = np.arange(DIL_SUB)[:, None]
    col = np.arange(2 * DIL_SUB)[None, :]
    band = (col - row >= 0) & (col - row <= 2 * HALF)
    variants = [band & (col >= HALF), band, band & (col < 2 * DIL_SUB - HALF)]
    tab = np.stack([np.where(np.concatenate([m, m], axis=0), 0.0, NEG) for m in variants])
    return jnp.asarray(tab, dtype=F32)


NA_GROUP = 8
NA_TOK = NA_GROUP * GRID_W
NA_KEYS = NA_ROWS * GRID_W
NA_UNROLL = 8


def _natten_kernel(q_ref, kp_ref, kc_ref, kn_ref, vp_ref, vc_ref, vn_ref, bias_ref,
                   o_ref, kbuf, vaug, *, n_rows):
    kbuf[0:NA_TOK] = kp_ref[...]
    kbuf[NA_TOK:2 * NA_TOK] = kc_ref[...]
    kbuf[2 * NA_TOK:3 * NA_TOK] = kn_ref[...]
    _fill_vaug(vaug, 0, vp_ref[...])
    _fill_vaug(vaug, NA_TOK, vc_ref[...])
    _fill_vaug(vaug, 2 * NA_TOK, vn_ref[...])
    g = pl.program_id(1)

    def body(j, carry):
        r = g * NA_GROUP + j
        r0 = jnp.clip(r - NA_ROWS // 2, 0, n_rows - NA_ROWS)
        off = r - r0
        start = pl.multiple_of((r0 - (g - 1) * NA_GROUP) * GRID_W, GRID_W)
        q0 = pl.multiple_of(j * GRID_W, GRID_W)
        for hp in range(WIDTH_B // LANES):
            o, _ = _attend_pair(q_ref[pl.ds(q0, GRID_W), LANES * hp:LANES * (hp + 1)],
                                kbuf[pl.ds(start, NA_KEYS), LANES * hp:LANES * (hp + 1)],
                                vaug[pl.ds(start, NA_KEYS), 2 * LANES * hp:2 * LANES * (hp + 1)],
                                bias_ref[off, hp])
            o_ref[pl.ds(q0, GRID_W), LANES * hp:LANES * (hp + 1)] = o.astype(BF16)
        return carry

    lax.fori_loop(0, NA_GROUP, body, 0, unroll=NA_UNROLL)


def _natten(q, k, v, bias):
    b, t, w = q.shape
    n_rows = t // GRID_W
    n_groups = n_rows // NA_GROUP
    blk = lambda f: pl.BlockSpec((None, NA_TOK, w), lambda bi, g: (bi, f(g), 0))
    cur = blk(lambda g: g)
    prev = blk(lambda g: jnp.maximum(g - 1, 0))
    nxt = blk(lambda g: jnp.minimum(g + 1, n_groups - 1))
    return pl.pallas_call(
        functools.partial(_natten_kernel, n_rows=n_rows),
        grid=(b, n_groups),
        in_specs=[cur, prev, cur, nxt, prev, cur, nxt, _const_spec(bias.shape)],
        out_specs=cur,
        out_shape=jax.ShapeDtypeStruct((b, t, w), BF16),
        scratch_shapes=[pltpu.VMEM((3 * NA_TOK, w), BF16), pltpu.VMEM((3 * NA_TOK, 2 * w), BF16)],
        compiler_params=_cparams(2),
        name="natten",
    )(q, k, k, k, v, v, v, bias)


def _natten_bias(rpb):
    off = np.arange(NA_ROWS)[:, None]
    drow = np.arange(NA_ROWS)[None, :] - off + NA_ROWS - 1
    c = np.arange(GRID_W)[:, None]
    kc = np.arange(GRID_W)[None, :]
    c0 = np.clip(c - NA_COLS // 2, 0, GRID_W - NA_COLS)
    valid = (kc >= c0) & (kc < c0 + NA_COLS)
    dcol = kc - c + NA_COLS - 1
    row_sel = (drow[..., None] == np.arange(2 * NA_ROWS - 1)).astype(np.float32)
    col_sel = (valid[..., None] & (dcol[..., None] == np.arange(2 * NA_COLS - 1))).astype(np.float32)
    tab = jnp.einsum("hrd,oir,ckd->ohcik", rpb.astype(F32), row_sel, col_sel,
                     precision=lax.Precision.HIGHEST)
    tab = jnp.where(valid[None, None, :, None, :], tab, NEG)
    return tab.reshape(NA_ROWS, N_HEADS_B // 2, 2 * GRID_W, NA_KEYS)


FIN_TM = 512
FF_CHUNK = 1024


def _final_kernel(x_ref, o1_ref, o4_ref, o16_ref, l1_ref, l4_ref, l16_ref, ob_ref, qm_ref,
                  km_ref, vm_ref, ga_ref, gb_ref, gm_ref, wo_ref, gf_ref, w1_ref, w2_ref, y_ref,
                  *stages):
    tm = x_ref.shape[0]

    n_tiles = WIDTH_A // LANES
    rows4, rows16 = tm // DIL_STEP, tm // (DIL_STEP * DIL_STEP)

    def token_order(ref, stage, stage_b=None):
        if stage_b is not None:
            for r4 in range(DIL_STEP):
                for q in range(DIL_STEP):
                    for c in range(n_tiles):
                        col = (r4 + DIL_STEP * q) * WIDTH_A + c * LANES
                        stage_b[c, pl.ds(r4 * rows4 + q, rows16, stride=DIL_STEP), :] = (
                            ref[:, col:col + LANES].astype(F32))
        for r4 in range(DIL_STEP):
            for c in range(n_tiles):
                if stage_b is not None:
                    cls = stage_b[c, r4 * rows4:(r4 + 1) * rows4, :]
                else:
                    col = r4 * WIDTH_A + c * LANES
                    cls = ref[:, col:col + LANES].astype(F32)
                stage[c, pl.ds(r4, rows4, stride=DIL_STEP), :] = cls
        return jnp.concatenate([stage[c] for c in range(n_tiles)], axis=1)

    o1, l1 = o1_ref[...].astype(F32), l1_ref[...]
    o4, l4 = token_order(o4_ref, stages[0]), token_order(l4_ref, stages[1])
    o16, l16 = token_order(o16_ref, stages[2], stages[4]), token_order(l16_ref, stages[3], stages[5])
    mx = jnp.maximum(jnp.maximum(l1, l4), l16)
    e1, e4, e16 = jnp.exp(l1 - mx), jnp.exp(l4 - mx), jnp.exp(l16 - mx)
    oa = (e1 * o1 + e4 * o4 + e16 * o16) / (e1 + e4 + e16)
    om = jnp.concatenate(
        [_attend_pair(qm_ref[:, LANES * hp:LANES * (hp + 1)], km_ref[:, LANES * hp:LANES * (hp + 1)],
                      vm_ref[:, 2 * LANES * hp:2 * LANES * (hp + 1)], None)[0]
         for hp in range(WIDTH_M // LANES)], axis=1)
    mixed = jnp.concatenate([_rms(oa, ga_ref[...]), _rms(ob_ref[...].astype(F32), gb_ref[...]),
                             _rms(om, gm_ref[...])], axis=1).astype(BF16)
    x1 = x_ref[...] + jnp.dot(mixed, wo_ref[...], preferred_element_type=F32)
    hf = _rms(x1, gf_ref[...]).astype(BF16)
    acc = x1
    for c in range(0, D_FF, FF_CHUNK):
        h = jnp.dot(hf, w1_ref[:, c:c + FF_CHUNK], preferred_element_type=F32)
        h = jnp.square(jnp.maximum(h, 0.0)).astype(BF16)
        acc = acc + jnp.dot(h, w2_ref[c:c + FF_CHUNK, :], preferred_element_type=F32)
    y_ref[...] = acc


def _final(x, o1, o4, o16, l1, l4, l16, ob, qm, km, vm, ga, gb, gm, wo, gf, w1, w2):
    b, t, _ = x.shape
    tm = FIN_TM
    tok = lambda width: pl.BlockSpec((None, tm, width), lambda bi, i: (bi, i, 0))
    per_b = lambda rows, width: pl.BlockSpec((None, rows, width), lambda bi, i: (bi, 0, 0))
    cls = [pl.BlockSpec((None, tm // d, d * WIDTH_A), lambda bi, i: (bi, i, 0)) for d in DILATIONS]
    return pl.pallas_call(
        _final_kernel,
        grid=(b, t // tm),
        in_specs=[tok(D_MODEL), *cls, *cls, tok(WIDTH_B), tok(WIDTH_M),
                  per_b(N_MEM, WIDTH_M), per_b(N_MEM, 2 * WIDTH_M),
                  _const_spec((1, WIDTH_A)), _const_spec((1, WIDTH_B)), _const_spec((1, WIDTH_M)),
                  _const_spec((D_MODEL, D_MODEL)), _const_spec((1, D_MODEL)),
                  _const_spec((D_MODEL, D_FF)), _const_spec((D_FF, D_MODEL))],
        out_specs=tok(D_MODEL),
        out_shape=jax.ShapeDtypeStruct((b, t, D_MODEL), F32),
        scratch_shapes=[pltpu.VMEM((WIDTH_A // LANES, tm, LANES), F32)] * 6,
        compiler_params=_cparams(2),
        name="final",
    )(x, o1, o4, o16, l1, l4, l16, ob, qm, km, vm, ga, gb, gm, wo, gf, w1, w2)


def _rope_tables(t):
    half = ROPE_DIMS // 2
    inv = ROPE_THETA ** (-(np.arange(half, dtype=np.float64) * 2.0 / ROPE_DIMS))
    ang = np.arange(t, dtype=np.float64)[:, None] * inv[None, :]
    cs = jnp.asarray(np.concatenate([np.cos(ang), np.sin(ang)], axis=1), dtype=F32)
    cos, sin = cs[:, :half], cs[:, half:]
    rest = HEAD_DIM - ROPE_DIMS
    cos_h = jnp.concatenate([cos, cos, jnp.ones((t, rest), F32)], axis=1)
    sin_h = jnp.concatenate([-sin, sin, jnp.zeros((t, rest), F32)], axis=1)
    return jnp.tile(cos_h, (1, LANES // HEAD_DIM)), jnp.tile(sin_h, (1, LANES // HEAD_DIM))


def _layer(x, mem, p, tables):
    cos_t, sin_t = tables[x.shape[1]]
    qa, ka, va, (qb, kb, vb, qm) = _proj(x, p["g_attn"], p["w_in"], p["bd"], p["gqa"], p["gka"],
                                         p["gqb"], p["gkb"], p["gqm"], cos_t, sin_t)
    km, vm = _memkv(mem, p["g_mem"], p["w_kv"], p["bd"], p["gkm"])
    pats = [_dilated(qa[n], ka[n], va[n], p["band"], d) for n, d in enumerate(DILATIONS)]
    ob = _natten(qb, kb, vb, p["na_bias"])
    return _final(x, pats[0][0], pats[1][0], pats[2][0], pats[0][1], pats[1][1], pats[2][1],
                  ob, qm, km, vm, p["ga"], p["gb"], p["gm"], p["w_out"], p["g_ffn"], p["w1"], p["w2"])


def kernel(x_prompt, x_sample, mem_prompt, mem_sample, norm_attn, w_in, q_norm_a, k_norm_a, q_norm_b, k_norm_b, rpb_b, norm_mem, w_mem_kv, q_norm_m, k_norm_m, out_norm_a, out_norm_b, out_norm_m, w_out, norm_ffn, w_ff1, w_ff2):
    depth = w_in.shape[0]
    scale = HEAD_DIM ** -0.5
    row = lambda v: v.astype(F32)[None, :]
    heads = lambda v, n: jnp.tile(v.astype(F32), n)[None, :]
    idx = np.arange(2 * LANES)
    bd = jnp.asarray(idx[:, None] // HEAD_DIM == idx[None, :] // HEAD_DIM, dtype=BF16)
    band = _band_bias()
    tables = {t: _rope_tables(t) for t in {x_prompt.shape[1], x_sample.shape[1]}}
    y_prompt, y_sample = x_prompt, x_sample
    for i in range(depth):
        p = dict(
            g_attn=row(norm_attn[i]), w_in=w_in[i].astype(BF16), bd=bd, band=band,
            gqa=heads(q_norm_a[i], N_HEADS_A) * scale, gka=heads(k_norm_a[i], N_HEADS_A),
            gqb=heads(q_norm_b[i], N_HEADS_B) * scale, gkb=heads(k_norm_b[i], N_HEADS_B),
            gqm=heads(q_norm_m[i], N_HEADS_M) * scale, gkm=heads(k_norm_m[i], N_HEADS_M),
            na_bias=_natten_bias(rpb_b[i]),
            g_mem=row(norm_mem[i]), w_kv=w_mem_kv[i].astype(BF16),
            ga=row(out_norm_a[i]), gb=row(out_norm_b[i]), gm=row(out_norm_m[i]),
            w_out=w_out[i].astype(BF16), g_ffn=row(norm_ffn[i]),
            w1=w_ff1[i].astype(BF16), w2=w_ff2[i].astype(BF16))
        y_prompt = _layer(y_prompt, mem_prompt, p, tables)
        y_sample = _layer(y_sample, mem_sample, p, tables)
    return (y_prompt, y_sample)
```

```python
import functools

import numpy as np
import jax
import jax.numpy as jnp
from jax import lax
from jax.experimental import pallas as pl
from jax.experimental.pallas import tpu as pltpu

D_MODEL = 1024
HEAD_DIM = 64
N_HEADS_A = 6
N_HEADS_B = 6
N_HEADS_M = 4
WIDTH_A = N_HEADS_A * HEAD_DIM
WIDTH_B = N_HEADS_B * HEAD_DIM
WIDTH_M = N_HEADS_M * HEAD_DIM
IN_WIDTH = 3 * WIDTH_A + 3 * WIDTH_B + WIDTH_M
DILATIONS = (1, 4, 16)
DIL_STEP = 4
HALF = 64
GRID_W = 64
NA_ROWS = 8
NA_COLS = 16
N_MEM = 256
D_FF = 4 * D_MODEL
ROPE_THETA = 500000.0
ROPE_DIMS = HEAD_DIM // 4
EPS = 1e-6
NEG = -1e30

LANES = 128
PAIR = 2 * HEAD_DIM
VMEM_LIMIT = 56 * 1024 * 1024

BF16 = jnp.bfloat16
F32 = jnp.float32


def _cparams(n_axes):
    return pltpu.CompilerParams(dimension_semantics=("parallel",) * n_axes,
                                vmem_limit_bytes=VMEM_LIMIT)


def _const_spec(shape):
    nd = len(shape)
    return pl.BlockSpec(shape, lambda *_: (0,) * nd, pipeline_mode=pl.Buffered(1))


def _rms(x, gain):
    return x * lax.rsqrt(jnp.mean(x * x, axis=-1, keepdims=True) + EPS) * gain


def _head_rms(z, bd, gain):
    ss = jnp.dot((z * z).astype(BF16), bd, preferred_element_type=F32)
    return z * lax.rsqrt(ss * (1.0 / HEAD_DIM) + EPS) * gain


def _attend_pair(qp, kp, vaug, bias):
    p, mx = _pair_probs(_pair_scores(qp, kp, bias))
    return _pair_output(p, mx, vaug)


def _pair_scores(qp, kp, bias):
    lane = lax.broadcasted_iota(jnp.int32, qp.shape, 1)
    zero = jnp.zeros_like(qp)
    lhs = jnp.concatenate([jnp.where(lane < HEAD_DIM, qp, zero),
                           jnp.where(lane >= HEAD_DIM, qp, zero)], axis=0)
    s = lax.dot_general(lhs, kp, (((1,), (1,)), ((), ())), preferred_element_type=F32)
    return s if bias is None else s + bias


def _pair_probs(s):
    mx = jnp.max(s, axis=-1, keepdims=True)
    return jnp.exp(s - mx).astype(BF16), mx


def _pair_output(p, mx, vaug):
    m_rows = p.shape[0] // 2
    ov = jnp.dot(p, vaug, preferred_element_type=F32)
    first = lax.broadcasted_iota(jnp.int32, (m_rows, LANES), 1) < HEAD_DIM
    o = jnp.where(first, ov[:m_rows, :LANES], ov[m_rows:, :LANES])
    l = jnp.where(first, ov[:m_rows, LANES:], ov[m_rows:, LANES:])
    mm = jnp.where(first, mx[:m_rows], mx[m_rows:])
    return o / l, mm + jnp.log(l)


def _fill_vaug(vaug_ref, row0, v):
    rows = v.shape[0]
    ones = jnp.ones((rows, LANES), BF16)
    for hp in range(v.shape[1] // LANES):
        vaug_ref[row0:row0 + rows, 2 * LANES * hp:2 * LANES * hp + LANES] = v[:, LANES * hp:LANES * (hp + 1)]
        vaug_ref[row0:row0 + rows, 2 * LANES * hp + LANES:2 * LANES * (hp + 1)] = ones


PROJ_TM = 512


def _proj_kernel(x_ref, g_ref, w_ref, bd_ref, gqka_ref, gqkb_ref, gqm_ref, cos_ref, sin_ref, *refs):
    n_dil = len(DILATIONS)
    qa_refs, ka_refs, va_refs = refs[:n_dil], refs[n_dil:2 * n_dil], refs[2 * n_dil:3 * n_dil]
    qb_ref, kb_ref, vb_ref, qm_ref = refs[3 * n_dil:3 * n_dil + 4]
    stages = refs[3 * n_dil + 4:]
    tm = x_ref.shape[0]
    x = x_ref[...]
    xn = _rms(x, g_ref[...]).astype(BF16)
    bd = bd_ref[...]

    def project(c0, width):
        return jnp.dot(xn, w_ref[:, c0:c0 + width], preferred_element_type=F32)

    def normed(z, gain_ref):
        return jnp.concatenate(
            [_head_rms(z[:, off:off + 2 * LANES], bd, gain_ref[:, off:off + 2 * LANES])
             for off in range(0, z.shape[1], 2 * LANES)], axis=1)

    def rope(y):
        cos = cos_ref[...]
        sin = sin_ref[...]
        lane = lax.broadcasted_iota(jnp.int32, cos.shape, 1) % HEAD_DIM
        low = lane < ROPE_DIMS // 2
        parts = []
        for c in range(0, y.shape[1], LANES):
            yc = y[:, c:c + LANES]
            partner = jnp.where(low, pltpu.roll(yc, LANES - ROPE_DIMS // 2, axis=1),
                                pltpu.roll(yc, ROPE_DIMS // 2, axis=1))
            parts.append(yc * cos + partner * sin)
        return jnp.concatenate(parts, axis=1)

    def emit_classes(y, out_refs, stage_a, stage_b):
        n_tiles = WIDTH_A // LANES
        rows4, rows16 = tm // DIL_STEP, tm // (DIL_STEP * DIL_STEP)

        def put(ref, r, c, val):
            col = (r * n_tiles + c) * LANES
            ref[:, col:col + LANES] = val.astype(BF16)

        for c in range(n_tiles):
            put(out_refs[0], 0, c, y[:, c * LANES:(c + 1) * LANES])
            stage_a[c] = y[:, c * LANES:(c + 1) * LANES]
        for r4 in range(DIL_STEP):
            for c in range(n_tiles):
                cls = stage_a[c, pl.ds(r4, rows4, stride=DIL_STEP), :]
                put(out_refs[1], r4, c, cls)
                stage_b[c, r4 * rows4:(r4 + 1) * rows4, :] = cls
        for r4 in range(DIL_STEP):
            for q in range(DIL_STEP):
                for c in range(n_tiles):
                    put(out_refs[2], r4 + DIL_STEP * q, c,
                        stage_b[c, pl.ds(r4 * rows4 + q, rows16, stride=DIL_STEP), :])

    wa, wb = WIDTH_A, WIDTH_B
    p_a = project(0, 2 * wa)
    p_b = project(2 * wa, 2 * wb)
    qk_a = rope(normed(p_a, gqka_ref))
    emit_classes(qk_a[:, :wa], qa_refs, stages[0], stages[1])
    emit_classes(qk_a[:, wa:], ka_refs, stages[2], stages[3])
    p_v = project(2 * wa + 2 * wb, wa + wb)
    qk_b = normed(p_b, gqkb_ref).astype(BF16)
    qb_ref[...] = qk_b[:, :wb]
    kb_ref[...] = qk_b[:, wb:]
    p_m = project(3 * wa + 3 * wb, WIDTH_M)
    emit_classes(p_v[:, :wa], va_refs, stages[4], stages[5])
    vb_ref[...] = p_v[:, wa:].astype(BF16)
    qm_ref[...] = normed(p_m, gqm_ref).astype(BF16)


def _proj(x, g, w_in, bd, gqka, gqkb, gqm, cos_t, sin_t):
    b, t, _ = x.shape
    tm = PROJ_TM
    tok = lambda width: pl.BlockSpec((None, tm, width), lambda bi, i: (bi, i, 0))
    cls = lambda dil, w: pl.BlockSpec((None, tm // dil, dil * w), lambda bi, i: (bi, i, 0))
    cls_shape = lambda dil, w: jax.ShapeDtypeStruct((b, t // dil, dil * w), BF16)
    cls_widths = (WIDTH_A, WIDTH_A, WIDTH_A)
    rest_widths = (WIDTH_B, WIDTH_B, WIDTH_B, WIDTH_M)
    n = len(DILATIONS)
    outs = pl.pallas_call(
        _proj_kernel,
        grid=(b, t // tm),
        in_specs=[tok(D_MODEL), _const_spec((1, D_MODEL)), _const_spec((D_MODEL, IN_WIDTH)),
                  _const_spec((2 * LANES, 2 * LANES)),
                  _const_spec((1, 2 * WIDTH_A)), _const_spec((1, 2 * WIDTH_B)), _const_spec((1, WIDTH_M)),
                  pl.BlockSpec((tm, LANES), lambda bi, i: (i, 0)),
                  pl.BlockSpec((tm, LANES), lambda bi, i: (i, 0))],
        out_specs=[cls(d, w) for w in cls_widths for d in DILATIONS] + [tok(w) for w in rest_widths],
        out_shape=[cls_shape(d, w) for w in cls_widths for d in DILATIONS]
                  + [jax.ShapeDtypeStruct((b, t, w), BF16) for w in rest_widths],
        scratch_shapes=[pltpu.VMEM((WIDTH_A // LANES, tm, LANES), F32)] * 6,
        compiler_params=_cparams(2),
        name="proj",
    )(x, g, w_in, bd, gqka, gqkb, gqm, cos_t, sin_t)
    return outs[:n], outs[n:2 * n], outs[2 * n:3 * n], outs[3 * n:]


def _memkv_kernel(mem_ref, g_ref, w_ref, bd_ref, gk_ref, km_ref, vaug_ref):
    mn = _rms(mem_ref[...], g_ref[...]).astype(BF16)
    kv = jnp.dot(mn, w_ref[...], preferred_element_type=F32)
    km_ref[...] = _head_rms(kv[:, :WIDTH_M], bd_ref[...], gk_ref[...]).astype(BF16)
    _fill_vaug(vaug_ref, 0, kv[:, WIDTH_M:].astype(BF16))


def _memkv(mem, g, w_kv, bd, gk):
    b = mem.shape[0]
    return pl.pallas_call(
        _memkv_kernel,
        grid=(b,),
        in_specs=[pl.BlockSpec((None, N_MEM, D_MODEL), lambda bi: (bi, 0, 0)),
                  _const_spec((1, D_MODEL)), _const_spec((D_MODEL, 2 * WIDTH_M)),
                  _const_spec((2 * LANES, 2 * LANES)), _const_spec((1, WIDTH_M))],
        out_specs=[pl.BlockSpec((None, N_MEM, WIDTH_M), lambda bi: (bi, 0, 0)),
                   pl.BlockSpec((None, N_MEM, 2 * WIDTH_M), lambda bi: (bi, 0, 0))],
        out_shape=[jax.ShapeDtypeStruct((b, N_MEM, WIDTH_M), BF16),
                   jax.ShapeDtypeStruct((b, N_MEM, 2 * WIDTH_M), BF16)],
        compiler_params=_cparams(1),
        name="memkv",
    )(mem, g, w_kv, bd, gk)


DIL_QB = 2048
DIL_SUB = 2 * HALF


def _dilated_kernel(q_ref, kp_ref, kc_ref, kn_ref, vp_ref, vc_ref, vn_ref, bias_ref,
                    o_ref, lse_ref, kbuf, vaug, *, n_sub_total):
    qb = q_ref.shape[0]
    kbuf[0:HALF] = kp_ref[...]
    kbuf[HALF:HALF + qb] = kc_ref[...]
    kbuf[HALF + qb:2 * HALF + qb] = kn_ref[...]
    _fill_vaug(vaug, 0, vp_ref[...])
    _fill_vaug(vaug, HALF, vc_ref[...])
    _fill_vaug(vaug, HALF + qb, vn_ref[...])
    n_sub = qb // DIL_SUB
    first_sub = pl.program_id(2) * n_sub

    for j in range(n_sub):
        gs = first_sub + j
        variant = jnp.where(gs == 0, 0, jnp.where(gs == n_sub_total - 1, 2, 1))
        r0 = j * DIL_SUB
        for hp in range(q_ref.shape[1] // LANES):
            o, lse = _attend_pair(q_ref[r0:r0 + DIL_SUB, LANES * hp:LANES * (hp + 1)],
                                  kbuf[r0:r0 + 2 * DIL_SUB, LANES * hp:LANES * (hp + 1)],
                                  vaug[r0:r0 + 2 * DIL_SUB, 2 * LANES * hp:2 * LANES * (hp + 1)],
                                  bias_ref[variant])
            o_ref[r0:r0 + DIL_SUB, LANES * hp:LANES * (hp + 1)] = o.astype(BF16)
            lse_ref[r0:r0 + DIL_SUB, LANES * hp:LANES * (hp + 1)] = lse


def _dilated(q, k, v, bias, dil):
    b, ln = q.shape[0], q.shape[1]
    qb = min(DIL_QB, ln)
    classes = min(dil, DIL_QB // qb)
    w = classes * (q.shape[2] // dil)
    hb = qb // HALF
    n_halo = ln // HALF
    main = pl.BlockSpec((None, qb, w), lambda bi, r, i: (bi, i, r))
    prev = pl.BlockSpec((None, HALF, w), lambda bi, r, i: (bi, jnp.maximum(i * hb - 1, 0), r))
    nxt = pl.BlockSpec((None, HALF, w), lambda bi, r, i: (bi, jnp.minimum((i + 1) * hb, n_halo - 1), r))
    o, lse = pl.pallas_call(
        functools.partial(_dilated_kernel, n_sub_total=ln // DIL_SUB),
        grid=(b, dil // classes, ln // qb),
        in_specs=[main, prev, main, nxt, prev, main, nxt, _const_spec(bias.shape)],
        out_specs=[main, main],
        out_shape=[jax.ShapeDtypeStruct(q.shape, BF16), jax.ShapeDtypeStruct(q.shape, F32)],
        scratch_shapes=[pltpu.VMEM((qb + 2 * HALF, w), BF16),
                        pltpu.VMEM((qb + 2 * HALF, 2 * w), BF16)],
        compiler_params=_cparams(3),
        name=f"dilated{dil}",
    )(q, k, k, k, v, v, v, bias)
    return o, lse


def _band_bias():
    row = np.arange(DIL_SUB)[:, None]
    col = np.arange(2 * DIL_SUB)[None, :]
    band = (col - row >= 0) & (col - row <= 2 * HALF)
    variants = [band & (col >= HALF), band, band & (col < 2 * DIL_SUB - HALF)]
    tab = np.stack([np.where(np.concatenate([m, m], axis=0), 0.0, NEG) for m in variants])
    return jnp.asarray(tab, dtype=F32)


NA_GROUP = 8
NA_TOK = NA_GROUP * GRID_W
NA_KEYS = NA_ROWS * GRID_W
NA_UNROLL = 8


def _natten_kernel(q_ref, kp_ref, kc_ref, kn_ref, vp_ref, vc_ref, vn_ref, bias_ref,
                   o_ref, kbuf, vaug, *, n_rows):
    kbuf[0:NA_TOK] = kp_ref[...]
    kbuf[NA_TOK:2 * NA_TOK] = kc_ref[...]
    kbuf[2 * NA_TOK:3 * NA_TOK] = kn_ref[...]
    _fill_vaug(vaug, 0, vp_ref[...])
    _fill_vaug(vaug, NA_TOK, vc_ref[...])
    _fill_vaug(vaug, 2 * NA_TOK, vn_ref[...])
    g = pl.program_id(1)

    def body(j, carry):
        r = g * NA_GROUP + j
        r0 = jnp.clip(r - NA_ROWS // 2, 0, n_rows - NA_ROWS)
        off = r - r0
        start = pl.multiple_of((r0 - (g - 1) * NA_GROUP) * GRID_W, GRID_W)
        q0 = pl.multiple_of(j * GRID_W, GRID_W)
        for hp in range(WIDTH_B // LANES):
            o, _ = _attend_pair(q_ref[pl.ds(q0, GRID_W), LANES * hp:LANES * (hp + 1)],
                                kbuf[pl.ds(start, NA_KEYS), LANES * hp:LANES * (hp + 1)],
                                vaug[pl.ds(start, NA_KEYS), 2 * LANES * hp:2 * LANES * (hp + 1)],
                                bias_ref[off, hp])
            o_ref[pl.ds(q0, GRID_W), LANES * hp:LANES * (hp + 1)] = o.astype(BF16)
        return carry

    lax.fori_loop(0, NA_GROUP, body, 0, unroll=NA_UNROLL)


def _natten(q, k, v, bias):
    b, t, w = q.shape
    n_rows = t // GRID_W
    n_groups = n_rows // NA_GROUP
    blk = lambda f: pl.BlockSpec((None, NA_TOK, w), lambda bi, g: (bi, f(g), 0))
    cur = blk(lambda g: g)
    prev = blk(lambda g: jnp.maximum(g - 1, 0))
    nxt = blk(lambda g: jnp.minimum(g + 1, n_groups - 1))
    return pl.pallas_call(
        functools.partial(_natten_kernel, n_rows=n_rows),
        grid=(b, n_groups),
        in_specs=[cur, prev, cur, nxt, prev, cur, nxt, _const_spec(bias.shape)],
        out_specs=cur,
        out_shape=jax.ShapeDtypeStruct((b, t, w), BF16),
        scratch_shapes=[pltpu.VMEM((3 * NA_TOK, w), BF16), pltpu.VMEM((3 * NA_TOK, 2 * w), BF16)],
        compiler_params=_cparams(2),
        name="natten",
    )(q, k, k, k, v, v, v, bias)


def _natten_bias(rpb):
    c = np.arange(GRID_W)[:, None]
    kc = np.arange(GRID_W)[None, :]
    c0 = np.clip(c - NA_COLS // 2, 0, GRID_W - NA_COLS)
    valid = (kc >= c0) & (kc < c0 + NA_COLS)
    dcol = kc - c + NA_COLS - 1
    col_sel = (valid[..., None] & (dcol[..., None] == np.arange(2 * NA_COLS - 1))).astype(np.float32)
    cols = jnp.einsum("hrd,ckd->hcrk", rpb.astype(F32), col_sel, precision=lax.Precision.HIGHEST)
    cols = jnp.where(valid[None, :, None, :], cols, NEG)
    cols = cols.reshape(N_HEADS_B // 2, 2 * GRID_W, (2 * NA_ROWS - 1) * GRID_W)
    tab = jnp.stack([cols[:, :, (NA_ROWS - 1 - off) * GRID_W:(2 * NA_ROWS - 1 - off) * GRID_W]
                     for off in range(NA_ROWS)])
    return tab


FIN_TM = 512
FF_CHUNK = 1024


def _final_kernel(x_ref, o1_ref, o4_ref, o16_ref, l1_ref, l4_ref, l16_ref, ob_ref, qm_ref,
                  km_ref, vm_ref, ga_ref, gb_ref, gm_ref, wo_ref, gf_ref, w1_ref, w2_ref, y_ref,
                  *stages):
    ts = x_ref.shape[0]
    base = 0
    n_tiles = WIDTH_A // LANES
    rows4, rows16 = ts // DIL_STEP, ts // (DIL_STEP * DIL_STEP)

    def token_order(ref, base, dil, stage, stage_b=None):
        src = lambda col: ref[base // dil:(base + ts) // dil, col:col + LANES].astype(F32)
        if stage_b is not None:
            for r4 in range(DIL_STEP):
                for q in range(DIL_STEP):
                    for c in range(n_tiles):
                        stage_b[c, pl.ds(base + r4 * rows4 + q, rows16, stride=DIL_STEP), :] = (
                            src((r4 + DIL_STEP * q) * WIDTH_A + c * LANES))
        for r4 in range(DIL_STEP):
            for c in range(n_tiles):
                if stage_b is not None:
                    cls = stage_b[c, base + r4 * rows4:base + (r4 + 1) * rows4, :]
                else:
                    cls = src(r4 * WIDTH_A + c * LANES)
                stage[c, pl.ds(base + r4, rows4, stride=DIL_STEP), :] = cls
        return jnp.concatenate([stage[c, base:base + ts, :] for c in range(n_tiles)], axis=1)

    o1, l1 = o1_ref[...].astype(F32), l1_ref[...]
    o4 = token_order(o4_ref, base, DILATIONS[1], stages[0])
    l4 = token_order(l4_ref, base, DILATIONS[1], stages[1])
    o16 = token_order(o16_ref, base, DILATIONS[2], stages[2], stages[4])
    l16 = token_order(l16_ref, base, DILATIONS[2], stages[3], stages[5])
    mx = jnp.maximum(jnp.maximum(l1, l4), l16)
    e1, e4, e16 = jnp.exp(l1 - mx), jnp.exp(l4 - mx), jnp.exp(l16 - mx)
    oa = (e1 * o1 + e4 * o4 + e16 * o16) / (e1 + e4 + e16)
    om = jnp.concatenate(
        [_attend_pair(qm_ref[:, LANES * hp:LANES * (hp + 1)], km_ref[:, LANES * hp:LANES * (hp + 1)],
                      vm_ref[:, 2 * LANES * hp:2 * LANES * (hp + 1)], None)[0]
         for hp in range(WIDTH_M // LANES)], axis=1)
    mixed = jnp.concatenate([_rms(oa, ga_ref[...]), _rms(ob_ref[...].astype(F32), gb_ref[...]),
                             _rms(om, gm_ref[...])], axis=1).astype(BF16)
    x1 = x_ref[...] + jnp.dot(mixed, wo_ref[...], preferred_element_type=F32)
    hf = _rms(x1, gf_ref[...]).astype(BF16)
    acc = x1
    for c in range(0, D_FF, FF_CHUNK):
        h = jnp.dot(hf, w1_ref[:, c:c + FF_CHUNK], preferred_element_type=F32)
        h = jnp.square(jnp.maximum(h, 0.0)).astype(BF16)
        acc = acc + jnp.dot(h, w2_ref[c:c + FF_CHUNK, :], preferred_element_type=F32)
    y_ref[...] = acc


def _final(x, o1, o4, o16, l1, l4, l16, ob, qm, km, vm, ga, gb, gm, wo, gf, w1, w2):
    b, t, _ = x.shape
    tm = FIN_TM
    tok = lambda width: pl.BlockSpec((None, tm, width), lambda bi, i: (bi, i, 0))
    per_b = lambda rows, width: pl.BlockSpec((None, rows, width), lambda bi, i: (bi, 0, 0))
    cls = [pl.BlockSpec((None, tm // d, d * WIDTH_A), lambda bi, i: (bi, i, 0)) for d in DILATIONS]
    return pl.pallas_call(
        _final_kernel,
        grid=(b, t // tm),
        in_specs=[tok(D_MODEL), *cls, *cls, tok(WIDTH_B), tok(WIDTH_M),
                  per_b(N_MEM, WIDTH_M), per_b(N_MEM, 2 * WIDTH_M),
                  _const_spec((1, WIDTH_A)), _const_spec((1, WIDTH_B)), _const_spec((1, WIDTH_M)),
                  _const_spec((D_MODEL, D_MODEL)), _const_spec((1, D_MODEL)),
                  _const_spec((D_MODEL, D_FF)), _const_spec((D_FF, D_MODEL))],
        out_specs=tok(D_MODEL),
        out_shape=jax.ShapeDtypeStruct((b, t, D_MODEL), F32),
        scratch_shapes=[pltpu.VMEM((WIDTH_A // LANES, tm, LANES), F32)] * 6,
        compiler_params=_cparams(2),
        name="final",
    )(x, o1, o4, o16, l1, l4, l16, ob, qm, km, vm, ga, gb, gm, wo, gf, w1, w2)


def _rope_tables(t):
    half = ROPE_DIMS // 2
    inv = ROPE_THETA ** (-(np.arange(half, dtype=np.float64) * 2.0 / ROPE_DIMS))
    ang = np.arange(t, dtype=np.float64)[:, None] * inv[None, :]
    cs = jnp.asarray(np.concatenate([np.cos(ang), np.sin(ang)], axis=1), dtype=F32)
    cos, sin = cs[:, :half], cs[:, half:]
    rest = HEAD_DIM - ROPE_DIMS
    cos_h = jnp.concatenate([cos, cos, jnp.ones((t, rest), F32)], axis=1)
    sin_h = jnp.concatenate([-sin, sin, jnp.zeros((t, rest), F32)], axis=1)
    return jnp.tile(cos_h, (1, LANES // HEAD_DIM)), jnp.tile(sin_h, (1, LANES // HEAD_DIM))


def _group_columns(w):
    a, b = WIDTH_A, WIDTH_B
    qa, ka, va = w[:, :a], w[:, a:2 * a], w[:, 2 * a:3 * a]
    qb, kb, vb = w[:, 3 * a:3 * a + b], w[:, 3 * a + b:3 * a + 2 * b], w[:, 3 * a + 2 * b:3 * a + 3 * b]
    return jnp.concatenate([qa, ka, qb, kb, va, vb, w[:, 3 * a + 3 * b:]], axis=1)


def _layer(x, mem, p, tables):
    cos_t, sin_t = tables[x.shape[1]]
    qa, ka, va, (qb, kb, vb, qm) = _proj(x, p["g_attn"], p["w_in"], p["bd"], p["gqka"], p["gqkb"],
                                         p["gqm"], cos_t, sin_t)
    km, vm = _memkv(mem, p["g_mem"], p["w_kv"], p["bd"], p["gkm"])
    pats = [_dilated(qa[n], ka[n], va[n], p["band"], d) for n, d in enumerate(DILATIONS)]
    ob = _natten(qb, kb, vb, p["na_bias"])
    return _final(x, pats[0][0], pats[1][0], pats[2][0], pats[0][1], pats[1][1], pats[2][1],
                  ob, qm, km, vm, p["ga"], p["gb"], p["gm"], p["w_out"], p["g_ffn"], p["w1"], p["w2"])


def kernel(x_prompt, x_sample, mem_prompt, mem_sample, norm_attn, w_in, q_norm_a, k_norm_a, q_norm_b, k_norm_b, rpb_b, norm_mem, w_mem_kv, q_norm_m, k_norm_m, out_norm_a, out_norm_b, out_norm_m, w_out, norm_ffn, w_ff1, w_ff2):
    depth = w_in.shape[0]
    scale = HEAD_DIM ** -0.5
    row = lambda v: v.astype(F32)[None, :]
    heads = lambda v, n: jnp.tile(v.astype(F32), n)[None, :]
    idx = np.arange(2 * LANES)
    bd = jnp.asarray(idx[:, None] // HEAD_DIM == idx[None, :] // HEAD_DIM, dtype=BF16)
    band = _band_bias()
    tables = {t: _rope_tables(t) for t in {x_prompt.shape[1], x_sample.shape[1]}}
    y_prompt, y_sample = x_prompt, x_sample
    for i in range(depth):
        p = dict(
            g_attn=row(norm_attn[i]), w_in=_group_columns(w_in[i]).astype(BF16), bd=bd, band=band,
            gqka=jnp.concatenate([heads(q_norm_a[i], N_HEADS_A) * scale, heads(k_norm_a[i], N_HEADS_A)], axis=1),
            gqkb=jnp.concatenate([heads(q_norm_b[i], N_HEADS_B) * scale, heads(k_norm_b[i], N_HEADS_B)], axis=1),
            gqm=heads(q_norm_m[i], N_HEADS_M) * scale, gkm=heads(k_norm_m[i], N_HEADS_M),
            na_bias=_natten_bias(rpb_b[i]),
            g_mem=row(norm_mem[i]), w_kv=w_mem_kv[i].astype(BF16),
            ga=row(out_norm_a[i]), gb=row(out_norm_b[i]), gm=row(out_norm_m[i]),
            w_out=w_out[i].astype(BF16), g_ffn=row(norm_ffn[i]),
            w1=w_ff1[i].astype(BF16), w2=w_ff2[i].astype(BF16))
        y_prompt = _layer(y_prompt, mem_prompt, p, tables)
        y_sample = _layer(y_sample, mem_sample, p, tables)
    return (y_prompt, y_sample)
```

```python
import functools

import numpy as np
import jax
import jax.numpy as jnp
from jax import lax
from jax.experimental import pallas as pl
from jax.experimental.pallas import tpu as pltpu

D_MODEL = 1024
HEAD_DIM = 64
N_HEADS_A = 6
N_HEADS_B = 6
N_HEADS_M = 4
WIDTH_A = N_HEADS_A * HEAD_DIM
WIDTH_B = N_HEADS_B * HEAD_DIM
WIDTH_M = N_HEADS_M * HEAD_DIM
IN_WIDTH = 3 * WIDTH_A + 3 * WIDTH_B + WIDTH_M
DILATIONS = (1, 4, 16)
DIL_STEP = 4
HALF = 64
GRID_W = 64
NA_ROWS = 8
NA_COLS = 16
N_MEM = 256
D_FF = 4 * D_MODEL
ROPE_THETA = 500000.0
ROPE_DIMS = HEAD_DIM // 4
EPS = 1e-6
NEG = -1e30
LOG2E = 1.4426950408889634
LN2 = 0.6931471805599453

LANES = 128
PAIR = 2 * HEAD_DIM
VMEM_LIMIT = 56 * 1024 * 1024

BF16 = jnp.bfloat16
F32 = jnp.float32


def _cparams(n_axes, in_order=False):
    return pltpu.CompilerParams(dimension_semantics=("arbitrary" if in_order else "parallel",) * n_axes,
                                vmem_limit_bytes=VMEM_LIMIT)


def _const_spec(shape):
    nd = len(shape)
    return pl.BlockSpec(shape, lambda *_: (0,) * nd, pipeline_mode=pl.Buffered(1))


def _rms(x, gain):
    return x * lax.rsqrt(jnp.mean(x * x, axis=-1, keepdims=True) + EPS) * gain


def _head_rms(z, bd, gain):
    ss = jnp.dot((z * z).astype(BF16), bd, preferred_element_type=F32)
    return z * lax.rsqrt(ss * (1.0 / HEAD_DIM) + EPS) * gain


def _attend_pair(qp, kp, vaug, bias):
    p, mx = _pair_probs(_pair_scores(qp, kp, bias))
    return _pair_output(p, mx, vaug)


def _pair_scores(qp, kp, bias):
    lane = lax.broadcasted_iota(jnp.int32, qp.shape, 1)
    zero = jnp.zeros_like(qp)
    lhs = jnp.concatenate([jnp.where(lane < HEAD_DIM, qp, zero),
                           jnp.where(lane >= HEAD_DIM, qp, zero)], axis=0)
    s = lax.dot_general(lhs, kp, (((1,), (1,)), ((), ())), preferred_element_type=F32)
    return s if bias is None else s + bias


def _pair_probs(s):
    mx = jnp.max(s, axis=-1, keepdims=True)
    return jnp.exp2(s - mx).astype(BF16), mx


def _pair_output(p, mx, vaug):
    m_rows = p.shape[0] // 2
    ov = jnp.dot(p, vaug, preferred_element_type=F32)
    first = lax.broadcasted_iota(jnp.int32, (m_rows, LANES), 1) < HEAD_DIM
    o = jnp.where(first, ov[:m_rows, :LANES], ov[m_rows:, :LANES])
    l = jnp.where(first, ov[:m_rows, LANES:], ov[m_rows:, LANES:])
    mm = jnp.where(first, mx[:m_rows], mx[m_rows:])
    return o / l, mm * LN2 + jnp.log(l)


def _fill_vaug(vaug_ref, row0, v, ones=True):
    rows = v.shape[0]
    for hp in range(v.shape[1] // LANES):
        vaug_ref[row0:row0 + rows, 2 * LANES * hp:2 * LANES * hp + LANES] = v[:, LANES * hp:LANES * (hp + 1)]
        if ones:
            vaug_ref[row0:row0 + rows, 2 * LANES * hp + LANES:2 * LANES * (hp + 1)] = jnp.ones((rows, LANES), BF16)


def _fill_ones(vaug_ref):
    for hp in range(vaug_ref.shape[1] // (2 * LANES)):
        vaug_ref[:, 2 * LANES * hp + LANES:2 * LANES * (hp + 1)] = jnp.ones((vaug_ref.shape[0], LANES), BF16)


def _first_step(n_axes):
    first = pl.program_id(0) == 0
    for axis in range(1, n_axes):
        first = first & (pl.program_id(axis) == 0)
    return first


PROJ_TM = 512


def _proj_kernel(x_ref, g_ref, w_ref, bd_ref, gqka_ref, gqkb_ref, gqm_ref, cos_ref, sin_ref, *refs):
    n_dil = len(DILATIONS)
    qa_refs, ka_refs, va_refs = refs[:n_dil], refs[n_dil:2 * n_dil], refs[2 * n_dil:3 * n_dil]
    qb_ref, kb_ref, vb_ref, qm_ref = refs[3 * n_dil:3 * n_dil + 4]
    stages = refs[3 * n_dil + 4:]
    tm = x_ref.shape[0]
    x = x_ref[...]
    xn = _rms(x, g_ref[...]).astype(BF16)
    bd = bd_ref[...]

    def project(c0, width):
        return jnp.dot(xn, w_ref[:, c0:c0 + width], preferred_element_type=F32)

    def normed(z, gain_ref):
        return jnp.concatenate(
            [_head_rms(z[:, off:off + 2 * LANES], bd, gain_ref[:, off:off + 2 * LANES])
             for off in range(0, z.shape[1], 2 * LANES)], axis=1)

    def rope(y):
        cos = cos_ref[...]
        sin = sin_ref[...]
        lane = lax.broadcasted_iota(jnp.int32, cos.shape, 1) % HEAD_DIM
        low = lane < ROPE_DIMS // 2
        parts = []
        for c in range(0, y.shape[1], LANES):
            yc = y[:, c:c + LANES]
            partner = jnp.where(low, pltpu.roll(yc, LANES - ROPE_DIMS // 2, axis=1),
                                pltpu.roll(yc, ROPE_DIMS // 2, axis=1))
            parts.append(yc * cos + partner * sin)
        return jnp.concatenate(parts, axis=1)

    def emit_classes(y, out_refs, stage_a, stage_b):
        n_tiles = WIDTH_A // LANES
        rows4, rows16 = tm // DIL_STEP, tm // (DIL_STEP * DIL_STEP)

        def put(ref, r, c, val):
            col = (r * n_tiles + c) * LANES
            ref[:, col:col + LANES] = val.astype(BF16)

        for c in range(n_tiles):
            put(out_refs[0], 0, c, y[:, c * LANES:(c + 1) * LANES])
            stage_a[c] = y[:, c * LANES:(c + 1) * LANES]
        for r4 in range(DIL_STEP):
            for c in range(n_tiles):
                cls = stage_a[c, pl.ds(r4, rows4, stride=DIL_STEP), :]
                put(out_refs[1], r4, c, cls)
                stage_b[c, r4 * rows4:(r4 + 1) * rows4, :] = cls
        for r4 in range(DIL_STEP):
            for q in range(DIL_STEP):
                for c in range(n_tiles):
                    put(out_refs[2], r4 + DIL_STEP * q, c,
                        stage_b[c, pl.ds(r4 * rows4 + q, rows16, stride=DIL_STEP), :])

    wa, wb = WIDTH_A, WIDTH_B
    p_a = project(0, 2 * wa)
    p_b = project(2 * wa, 2 * wb)
    qk_a = rope(normed(p_a, gqka_ref))
    emit_classes(qk_a[:, :wa], qa_refs, stages[0], stages[1])
    emit_classes(qk_a[:, wa:], ka_refs, stages[2], stages[3])
    p_v = project(2 * wa + 2 * wb, wa + wb)
    qk_b = normed(p_b, gqkb_ref).astype(BF16)
    qb_ref[...] = qk_b[:, :wb]
    kb_ref[...] = qk_b[:, wb:]
    p_m = project(3 * wa + 3 * wb, WIDTH_M)
    emit_classes(p_v[:, :wa], va_refs, stages[4], stages[5])
    vb_ref[...] = p_v[:, wa:].astype(BF16)
    qm_ref[...] = normed(p_m, gqm_ref).astype(BF16)


def _proj(x, g, w_in, bd, gqka, gqkb, gqm, cos_t, sin_t):
    b, t, _ = x.shape
    tm = PROJ_TM
    tok = lambda width: pl.BlockSpec((None, tm, width), lambda bi, i: (bi, i, 0))
    cls = lambda dil, w: pl.BlockSpec((None, tm // dil, dil * w), lambda bi, i: (bi, i, 0))
    cls_shape = lambda dil, w: jax.ShapeDtypeStruct((b, t // dil, dil * w), BF16)
    cls_widths = (WIDTH_A, WIDTH_A, WIDTH_A)
    rest_widths = (WIDTH_B, WIDTH_B, WIDTH_B, WIDTH_M)
    n = len(DILATIONS)
    outs = pl.pallas_call(
        _proj_kernel,
        grid=(b, t // tm),
        in_specs=[tok(D_MODEL), _const_spec((1, D_MODEL)), _const_spec((D_MODEL, IN_WIDTH)),
                  _const_spec((2 * LANES, 2 * LANES)),
                  _const_spec((1, 2 * WIDTH_A)), _const_spec((1, 2 * WIDTH_B)), _const_spec((1, WIDTH_M)),
                  pl.BlockSpec((tm, LANES), lambda bi, i: (i, 0)),
                  pl.BlockSpec((tm, LANES), lambda bi, i: (i, 0))],
        out_specs=[cls(d, w) for w in cls_widths for d in DILATIONS] + [tok(w) for w in rest_widths],
        out_shape=[cls_shape(d, w) for w in cls_widths for d in DILATIONS]
                  + [jax.ShapeDtypeStruct((b, t, w), BF16) for w in rest_widths],
        scratch_shapes=[pltpu.VMEM((WIDTH_A // LANES, tm, LANES), F32)] * 6,
        compiler_params=_cparams(2),
        name="proj",
    )(x, g, w_in, bd, gqka, gqkb, gqm, cos_t, sin_t)
    return outs[:n], outs[n:2 * n], outs[2 * n:3 * n], outs[3 * n:]


def _memkv_kernel(mem_ref, g_ref, w_ref, bd_ref, gk_ref, km_ref, vaug_ref):
    mn = _rms(mem_ref[...], g_ref[...]).astype(BF16)
    kv = jnp.dot(mn, w_ref[...], preferred_element_type=F32)
    km_ref[...] = _head_rms(kv[:, :WIDTH_M], bd_ref[...], gk_ref[...]).astype(BF16)
    _fill_vaug(vaug_ref, 0, kv[:, WIDTH_M:].astype(BF16))


def _memkv(mem, g, w_kv, bd, gk):
    b = mem.shape[0]
    return pl.pallas_call(
        _memkv_kernel,
        grid=(b,),
        in_specs=[pl.BlockSpec((None, N_MEM, D_MODEL), lambda bi: (bi, 0, 0)),
                  _const_spec((1, D_MODEL)), _const_spec((D_MODEL, 2 * WIDTH_M)),
                  _const_spec((2 * LANES, 2 * LANES)), _const_spec((1, WIDTH_M))],
        out_specs=[pl.BlockSpec((None, N_MEM, WIDTH_M), lambda bi: (bi, 0, 0)),
                   pl.BlockSpec((None, N_MEM, 2 * WIDTH_M), lambda bi: (bi, 0, 0))],
        out_shape=[jax.ShapeDtypeStruct((b, N_MEM, WIDTH_M), BF16),
                   jax.ShapeDtypeStruct((b, N_MEM, 2 * WIDTH_M), BF16)],
        compiler_params=_cparams(1),
        name="memkv",
    )(mem, g, w_kv, bd, gk)


DIL_QB = 2048
DIL_SUB = 2 * HALF


def _dilated_kernel(q_ref, kp_ref, kc_ref, kn_ref, vp_ref, vc_ref, vn_ref, bias_ref,
                    o_ref, lse_ref, kbuf, vaug, *, n_sub_total):
    qb = q_ref.shape[0]
    kbuf[0:HALF] = kp_ref[...]
    kbuf[HALF:HALF + qb] = kc_ref[...]
    kbuf[HALF + qb:2 * HALF + qb] = kn_ref[...]
    _fill_vaug(vaug, 0, vp_ref[...])
    _fill_vaug(vaug, HALF, vc_ref[...])
    _fill_vaug(vaug, HALF + qb, vn_ref[...])
    n_sub = qb // DIL_SUB
    first_sub = pl.program_id(2) * n_sub

    for j in range(n_sub):
        gs = first_sub + j
        variant = jnp.where(gs == 0, 0, jnp.where(gs == n_sub_total - 1, 2, 1))
        r0 = j * DIL_SUB
        for hp in range(q_ref.shape[1] // LANES):
            o, lse = _attend_pair(q_ref[r0:r0 + DIL_SUB, LANES * hp:LANES * (hp + 1)],
                                  kbuf[r0:r0 + 2 * DIL_SUB, LANES * hp:LANES * (hp + 1)],
                                  vaug[r0:r0 + 2 * DIL_SUB, 2 * LANES * hp:2 * LANES * (hp + 1)],
                                  bias_ref[variant])
            o_ref[r0:r0 + DIL_SUB, LANES * hp:LANES * (hp + 1)] = o.astype(BF16)
            lse_ref[r0:r0 + DIL_SUB, LANES * hp:LANES * (hp + 1)] = lse


def _dilated(q, k, v, bias, dil):
    b, ln = q.shape[0], q.shape[1]
    qb = min(DIL_QB, ln)
    classes = min(dil, DIL_QB // qb)
    w = classes * (q.shape[2] // dil)
    hb = qb // HALF
    n_halo = ln // HALF
    main = pl.BlockSpec((None, qb, w), lambda bi, r, i: (bi, i, r))
    prev = pl.BlockSpec((None, HALF, w), lambda bi, r, i: (bi, jnp.maximum(i * hb - 1, 0), r))
    nxt = pl.BlockSpec((None, HALF, w), lambda bi, r, i: (bi, jnp.minimum((i + 1) * hb, n_halo - 1), r))
    o, lse = pl.pallas_call(
        functools.partial(_dilated_kernel, n_sub_total=ln // DIL_SUB),
        grid=(b, dil // classes, ln // qb),
        in_specs=[main, prev, main, nxt, prev, main, nxt, _const_spec(bias.shape)],
        out_specs=[main, main],
        out_shape=[jax.ShapeDtypeStruct(q.shape, BF16), jax.ShapeDtypeStruct(q.shape, F32)],
        scratch_shapes=[pltpu.VMEM((qb + 2 * HALF, w), BF16),
                        pltpu.VMEM((qb + 2 * HALF, 2 * w), BF16)],
        compiler_params=_cparams(3),
        name=f"dilated{dil}",
    )(q, k, k, k, v, v, v, bias)
    return o, lse


def _band_bias():
    row = np.arange(DIL_SUB)[:, None]
    col = np.arange(2 * DIL_SUB)[None, :]
    band = (col - row >= 0) & (col - row <= 2 * HALF)
    variants = [band & (col >= HALF), band, band & (col < 2 * DIL_SUB - HALF)]
    tab = np.stack([np.where(np.concatenate([m, m], axis=0), 0.0, NEG) for m in variants])
    return jnp.asarray(tab, dtype=F32)


NA_GROUP = 8
NA_TOK = NA_GROUP * GRID_W
NA_KEYS = NA_ROWS * GRID_W
NA_UNROLL = 8


def _natten_kernel(q_ref, kp_ref, kc_ref, kn_ref, vp_ref, vc_ref, vn_ref, bias_ref,
                   o_ref, kbuf, vaug, *, n_rows):
    kbuf[0:NA_TOK] = kp_ref[...]
    kbuf[NA_TOK:2 * NA_TOK] = kc_ref[...]
    kbuf[2 * NA_TOK:3 * NA_TOK] = kn_ref[...]
    pl.when(_first_step(2))(lambda: _fill_ones(vaug))
    _fill_vaug(vaug, 0, vp_ref[...], ones=False)
    _fill_vaug(vaug, NA_TOK, vc_ref[...], ones=False)
    _fill_vaug(vaug, 2 * NA_TOK, vn_ref[...], ones=False)
    g = pl.program_id(1)

    def body(j, carry):
        r = g * NA_GROUP + j
        r0 = jnp.clip(r - NA_ROWS // 2, 0, n_rows - NA_ROWS)
        off = r - r0
        start = pl.multiple_of((r0 - (g - 1) * NA_GROUP) * GRID_W, GRID_W)
        q0 = pl.multiple_of(j * GRID_W, GRID_W)
        for hp in range(WIDTH_B // LANES):
            o, _ = _attend_pair(q_ref[pl.ds(q0, GRID_W), LANES * hp:LANES * (hp + 1)],
                                kbuf[pl.ds(start, NA_KEYS), LANES * hp:LANES * (hp + 1)],
                                vaug[pl.ds(start, NA_KEYS), 2 * LANES * hp:2 * LANES * (hp + 1)],
                                bias_ref[off, hp])
            o_ref[pl.ds(q0, GRID_W), LANES * hp:LANES * (hp + 1)] = o.astype(BF16)
        return carry

    lax.fori_loop(0, NA_GROUP, body, 0, unroll=NA_UNROLL)


def _natten(q, k, v, bias):
    b, t, w = q.shape
    n_rows = t // GRID_W
    n_groups = n_rows // NA_GROUP
    blk = lambda f: pl.BlockSpec((None, NA_TOK, w), lambda bi, g: (bi, f(g), 0))
    cur = blk(lambda g: g)
    prev = blk(lambda g: jnp.maximum(g - 1, 0))
    nxt = blk(lambda g: jnp.minimum(g + 1, n_groups - 1))
    return pl.pallas_call(
        functools.partial(_natten_kernel, n_rows=n_rows),
        grid=(b, n_groups),
        in_specs=[cur, prev, cur, nxt, prev, cur, nxt, _const_spec(bias.shape)],
        out_specs=cur,
        out_shape=jax.ShapeDtypeStruct((b, t, w), BF16),
        scratch_shapes=[pltpu.VMEM((3 * NA_TOK, w), BF16), pltpu.VMEM((3 * NA_TOK, 2 * w), BF16)],
        compiler_params=_cparams(2, in_order=True),
        name="natten",
    )(q, k, k, k, v, v, v, bias)


def _natten_bias(rpb):
    c = np.arange(GRID_W)[:, None]
    kc = np.arange(GRID_W)[None, :]
    c0 = np.clip(c - NA_COLS // 2, 0, GRID_W - NA_COLS)
    valid = (kc >= c0) & (kc < c0 + NA_COLS)
    dcol = kc - c + NA_COLS - 1
    col_sel = (valid[..., None] & (dcol[..., None] == np.arange(2 * NA_COLS - 1))).astype(np.float32)
    cols = jnp.einsum("hrd,ckd->hcrk", rpb.astype(F32), col_sel, precision=lax.Precision.HIGHEST)
    cols = jnp.where(valid[None, :, None, :], cols * LOG2E, NEG)
    cols = cols.reshape(N_HEADS_B // 2, 2 * GRID_W, (2 * NA_ROWS - 1) * GRID_W)
    tab = jnp.stack([cols[:, :, (NA_ROWS - 1 - off) * GRID_W:(2 * NA_ROWS - 1 - off) * GRID_W]
                     for off in range(NA_ROWS)])
    return tab


FIN_TM = 512
FF_CHUNK = 1024


def _final_kernel(x_ref, o1_ref, o4_ref, o16_ref, l1_ref, l4_ref, l16_ref, ob_ref, qm_ref,
                  km_ref, vm_ref, ga_ref, gb_ref, gm_ref, wo_ref, gf_ref, w1_ref, w2_ref, y_ref,
                  *stages):
    ts = x_ref.shape[0]
    base = 0
    n_tiles = WIDTH_A // LANES
    rows4, rows16 = ts // DIL_STEP, ts // (DIL_STEP * DIL_STEP)

    def token_order(ref, base, dil, stage, stage_b=None):
        src = lambda col: ref[base // dil:(base + ts) // dil, col:col + LANES].astype(F32)
        if stage_b is not None:
            for r4 in range(DIL_STEP):
                for q in range(DIL_STEP):
                    for c in range(n_tiles):
                        stage_b[c, pl.ds(base + r4 * rows4 + q, rows16, stride=DIL_STEP), :] = (
                            src((r4 + DIL_STEP * q) * WIDTH_A + c * LANES))
        for r4 in range(DIL_STEP):
            for c in range(n_tiles):
                if stage_b is not None:
                    cls = stage_b[c, base + r4 * rows4:base + (r4 + 1) * rows4, :]
                else:
                    cls = src(r4 * WIDTH_A + c * LANES)
                stage[c, pl.ds(base + r4, rows4, stride=DIL_STEP), :] = cls
        return jnp.concatenate([stage[c, base:base + ts, :] for c in range(n_tiles)], axis=1)

    o1, l1 = o1_ref[...].astype(F32), l1_ref[...]
    o4 = token_order(o4_ref, base, DILATIONS[1], stages[0])
    l4 = token_order(l4_ref, base, DILATIONS[1], stages[1])
    o16 = token_order(o16_ref, base, DILATIONS[2], stages[2], stages[4])
    l16 = token_order(l16_ref, base, DILATIONS[2], stages[3], stages[5])
    mx = jnp.maximum(jnp.maximum(l1, l4), l16)
    e1, e4, e16 = jnp.exp(l1 - mx), jnp.exp(l4 - mx), jnp.exp(l16 - mx)
    oa = (e1 * o1 + e4 * o4 + e16 * o16) / (e1 + e4 + e16)
    om = jnp.concatenate(
        [_attend_pair(qm_ref[:, LANES * hp:LANES * (hp + 1)], km_ref[:, LANES * hp:LANES * (hp + 1)],
                      vm_ref[:, 2 * LANES * hp:2 * LANES * (hp + 1)], None)[0]
         for hp in range(WIDTH_M // LANES)], axis=1)
    mixed = jnp.concatenate([_rms(oa, ga_ref[...]), _rms(ob_ref[...].astype(F32), gb_ref[...]),
                             _rms(om, gm_ref[...])], axis=1).astype(BF16)
    x1 = x_ref[...] + jnp.dot(mixed, wo_ref[...], preferred_element_type=F32)
    hf = _rms(x1, gf_ref[...]).astype(BF16)
    acc = x1
    for c in range(0, D_FF, FF_CHUNK):
        h = jnp.dot(hf, w1_ref[:, c:c + FF_CHUNK], preferred_element_type=F32)
        h = jnp.square(jnp.maximum(h, 0.0)).astype(BF16)
        acc = acc + jnp.dot(h, w2_ref[c:c + FF_CHUNK, :], preferred_element_type=F32)
    y_ref[...] = acc


def _final(x, o1, o4, o16, l1, l4, l16, ob, qm, km, vm, ga, gb, gm, wo, gf, w1, w2):
    b, t, _ = x.shape
    tm = FIN_TM
    tok = lambda width: pl.BlockSpec((None, tm, width), lambda bi, i: (bi, i, 0))
    per_b = lambda rows, width: pl.BlockSpec((None, rows, width), lambda bi, i: (bi, 0, 0))
    cls = [pl.BlockSpec((None, tm // d, d * WIDTH_A), lambda bi, i: (bi, i, 0)) for d in DILATIONS]
    return pl.pallas_call(
        _final_kernel,
        grid=(b, t // tm),
        in_specs=[tok(D_MODEL), *cls, *cls, tok(WIDTH_B), tok(WIDTH_M),
                  per_b(N_MEM, WIDTH_M), per_b(N_MEM, 2 * WIDTH_M),
                  _const_spec((1, WIDTH_A)), _const_spec((1, WIDTH_B)), _const_spec((1, WIDTH_M)),
                  _const_spec((D_MODEL, D_MODEL)), _const_spec((1, D_MODEL)),
                  _const_spec((D_MODEL, D_FF)), _const_spec((D_FF, D_MODEL))],
        out_specs=tok(D_MODEL),
        out_shape=jax.ShapeDtypeStruct((b, t, D_MODEL), F32),
        scratch_shapes=[pltpu.VMEM((WIDTH_A // LANES, tm, LANES), F32)] * 6,
        compiler_params=_cparams(2),
        name="final",
    )(x, o1, o4, o16, l1, l4, l16, ob, qm, km, vm, ga, gb, gm, wo, gf, w1, w2)


def _rope_tables(t):
    half = ROPE_DIMS // 2
    inv = ROPE_THETA ** (-(np.arange(half, dtype=np.float64) * 2.0 / ROPE_DIMS))
    ang = np.arange(t, dtype=np.float64)[:, None] * inv[None, :]
    cs = jnp.asarray(np.concatenate([np.cos(ang), np.sin(ang)], axis=1), dtype=F32)
    cos, sin = cs[:, :half], cs[:, half:]
    rest = HEAD_DIM - ROPE_DIMS
    cos_h = jnp.concatenate([cos, cos, jnp.ones((t, rest), F32)], axis=1)
    sin_h = jnp.concatenate([-sin, sin, jnp.zeros((t, rest), F32)], axis=1)
    return jnp.tile(cos_h, (1, LANES // HEAD_DIM)), jnp.tile(sin_h, (1, LANES // HEAD_DIM))


def _group_columns(w):
    a, b = WIDTH_A, WIDTH_B
    qa, ka, va = w[:, :a], w[:, a:2 * a], w[:, 2 * a:3 * a]
    qb, kb, vb = w[:, 3 * a:3 * a + b], w[:, 3 * a + b:3 * a + 2 * b], w[:, 3 * a + 2 * b:3 * a + 3 * b]
    return jnp.concatenate([qa, ka, qb, kb, va, vb, w[:, 3 * a + 3 * b:]], axis=1)


def _layer(x, mem, p, tables):
    cos_t, sin_t = tables[x.shape[1]]
    qa, ka, va, (qb, kb, vb, qm) = _proj(x, p["g_attn"], p["w_in"], p["bd"], p["gqka"], p["gqkb"],
                                         p["gqm"], cos_t, sin_t)
    km, vm = _memkv(mem, p["g_mem"], p["w_kv"], p["bd"], p["gkm"])
    pats = [_dilated(qa[n], ka[n], va[n], p["band"], d) for n, d in enumerate(DILATIONS)]
    ob = _natten(qb, kb, vb, p["na_bias"])
    return _final(x, pats[0][0], pats[1][0], pats[2][0], pats[0][1], pats[1][1], pats[2][1],
                  ob, qm, km, vm, p["ga"], p["gb"], p["gm"], p["w_out"], p["g_ffn"], p["w1"], p["w2"])


def kernel(x_prompt, x_sample, mem_prompt, mem_sample, norm_attn, w_in, q_norm_a, k_norm_a, q_norm_b, k_norm_b, rpb_b, norm_mem, w_mem_kv, q_norm_m, k_norm_m, out_norm_a, out_norm_b, out_norm_m, w_out, norm_ffn, w_ff1, w_ff2):
    depth = w_in.shape[0]
    scale = HEAD_DIM ** -0.5 * LOG2E
    row = lambda v: v.astype(F32)[None, :]
    heads = lambda v, n: jnp.tile(v.astype(F32), n)[None, :]
    idx = np.arange(2 * LANES)
    bd = jnp.asarray(idx[:, None] // HEAD_DIM == idx[None, :] // HEAD_DIM, dtype=BF16)
    band = _band_bias()
    tables = {t: _rope_tables(t) for t in {x_prompt.shape[1], x_sample.shape[1]}}
    y_prompt, y_sample = x_prompt, x_sample
    for i in range(depth):
        p = dict(
            g_attn=row(norm_attn[i]), w_in=_group_columns(w_in[i]).astype(BF16), bd=bd, band=band,
            gqka=jnp.concatenate([heads(q_norm_a[i], N_HEADS_A) * scale, heads(k_norm_a[i], N_HEADS_A)], axis=1),
            gqkb=jnp.concatenate([heads(q_norm_b[i], N_HEADS_B) * scale, heads(k_norm_b[i], N_HEADS_B)], axis=1),
            gqm=heads(q_norm_m[i], N_HEADS_M) * scale, gkm=heads(k_norm_m[i], N_HEADS_M),
            na_bias=_natten_bias(rpb_b[i]),
            g_mem=row(norm_mem[i]), w_kv=w_mem_kv[i].astype(BF16),
            ga=row(out_norm_a[i]), gb=row(out_norm_b[i]), gm=row(out_norm_m[i]),
            w_out=w_out[i].astype(BF16), g_ffn=row(norm_ffn[i]),
            w1=w_ff1[i].astype(BF16), w2=w_ff2[i].astype(BF16))
        y_prompt = _layer(y_prompt, mem_prompt, p, tables)
        y_sample = _layer(y_sample, mem_sample, p, tables)
    return (y_prompt, y_sample)
```

```python
import functools

import numpy as np
import jax
import jax.numpy as jnp
from jax import lax
from jax.experimental import pallas as pl
from jax.experimental.pallas import tpu as pltpu

D_MODEL = 1024
HEAD_DIM = 64
N_HEADS_A = 6
N_HEADS_B = 6
N_HEADS_M = 4
WIDTH_A = N_HEADS_A * HEAD_DIM
WIDTH_B = N_HEADS_B * HEAD_DIM
WIDTH_M = N_HEADS_M * HEAD_DIM
IN_WIDTH = 3 * WIDTH_A + 3 * WIDTH_B + WIDTH_M
DILATIONS = (1, 4, 16)
DIL_STEP = 4
HALF = 64
GRID_W = 64
NA_ROWS = 8
NA_COLS = 16
N_MEM = 256
D_FF = 4 * D_MODEL
ROPE_THETA = 500000.0
ROPE_DIMS = HEAD_DIM // 4
EPS = 1e-6
NEG = -1e30
LOG2E = 1.4426950408889634
LN2 = 0.6931471805599453

LANES = 128
PAIR = 2 * HEAD_DIM
VMEM_LIMIT = 56 * 1024 * 1024

BF16 = jnp.bfloat16
F32 = jnp.float32


def _cparams(n_axes, in_order=False):
    return pltpu.CompilerParams(dimension_semantics=("arbitrary" if in_order else "parallel",) * n_axes,
                                vmem_limit_bytes=VMEM_LIMIT)


def _const_spec(shape):
    nd = len(shape)
    return pl.BlockSpec(shape, lambda *_: (0,) * nd, pipeline_mode=pl.Buffered(1))


def _rms(x, gain):
    return x * lax.rsqrt(jnp.mean(x * x, axis=-1, keepdims=True) + EPS) * gain


def _head_rms(z, bd, gain):
    ss = jnp.dot((z * z).astype(BF16), bd, preferred_element_type=F32)
    return z * lax.rsqrt(ss * (1.0 / HEAD_DIM) + EPS) * gain


def _attend_pair(qp, kp, vaug, bias):
    p, mx = _pair_probs(_pair_scores(qp, kp, bias))
    return _pair_output(p, mx, vaug)


def _pair_scores(qp, kp, bias):
    lane = lax.broadcasted_iota(jnp.int32, qp.shape, 1)
    zero = jnp.zeros_like(qp)
    lhs = jnp.concatenate([jnp.where(lane < HEAD_DIM, qp, zero),
                           jnp.where(lane >= HEAD_DIM, qp, zero)], axis=0)
    s = lax.dot_general(lhs, kp, (((1,), (1,)), ((), ())), preferred_element_type=F32)
    return s if bias is None else s + bias


def _pair_probs(s):
    mx = jnp.max(s, axis=-1, keepdims=True)
    return jnp.exp2(s - mx).astype(BF16), mx


def _pair_output(p, mx, vaug):
    m_rows = p.shape[0] // 2
    ov = jnp.dot(p, vaug, preferred_element_type=F32)
    first = lax.broadcasted_iota(jnp.int32, (m_rows, LANES), 1) < HEAD_DIM
    o = jnp.where(first, ov[:m_rows, :LANES], ov[m_rows:, :LANES])
    l = jnp.where(first, ov[:m_rows, LANES:], ov[m_rows:, LANES:])
    mm = jnp.where(first, mx[:m_rows], mx[m_rows:])
    return o / l, mm * LN2 + jnp.log(l)


def _fill_vaug(vaug_ref, row0, v, ones=True):
    rows = v.shape[0]
    for hp in range(v.shape[1] // LANES):
        vaug_ref[row0:row0 + rows, 2 * LANES * hp:2 * LANES * hp + LANES] = v[:, LANES * hp:LANES * (hp + 1)]
        if ones:
            vaug_ref[row0:row0 + rows, 2 * LANES * hp + LANES:2 * LANES * (hp + 1)] = jnp.ones((rows, LANES), BF16)


def _fill_ones(vaug_ref):
    for hp in range(vaug_ref.shape[1] // (2 * LANES)):
        vaug_ref[:, 2 * LANES * hp + LANES:2 * LANES * (hp + 1)] = jnp.ones((vaug_ref.shape[0], LANES), BF16)


def _first_step(n_axes):
    first = pl.program_id(0) == 0
    for axis in range(1, n_axes):
        first = first & (pl.program_id(axis) == 0)
    return first


PROJ_TM = 1024


def _proj_kernel(x_ref, g_ref, w_ref, bd_ref, gqka_ref, gqkb_ref, gqm_ref, cos_ref, sin_ref, *refs):
    n_dil = len(DILATIONS)
    qa_refs, ka_refs, va_refs = refs[:n_dil], refs[n_dil:2 * n_dil], refs[2 * n_dil:3 * n_dil]
    qb_ref, kb_ref, vb_ref, qm_ref = refs[3 * n_dil:3 * n_dil + 4]
    stages = refs[3 * n_dil + 4:]
    tm = x_ref.shape[0]
    x = x_ref[...]
    xn = _rms(x, g_ref[...]).astype(BF16)
    bd = bd_ref[...]

    def project(c0, width):
        return jnp.dot(xn, w_ref[:, c0:c0 + width], preferred_element_type=F32)

    def normed(z, gain_ref):
        return jnp.concatenate(
            [_head_rms(z[:, off:off + 2 * LANES], bd, gain_ref[:, off:off + 2 * LANES])
             for off in range(0, z.shape[1], 2 * LANES)], axis=1)

    def rope(y):
        cos = cos_ref[...]
        sin = sin_ref[...]
        lane = lax.broadcasted_iota(jnp.int32, cos.shape, 1) % HEAD_DIM
        low = lane < ROPE_DIMS // 2
        parts = []
        for c in range(0, y.shape[1], LANES):
            yc = y[:, c:c + LANES]
            partner = jnp.where(low, pltpu.roll(yc, LANES - ROPE_DIMS // 2, axis=1),
                                pltpu.roll(yc, ROPE_DIMS // 2, axis=1))
            parts.append(yc * cos + partner * sin)
        return jnp.concatenate(parts, axis=1)

    def emit_classes(y, out_refs, stage_a, stage_b):
        n_tiles = WIDTH_A // LANES
        rows4, rows16 = tm // DIL_STEP, tm // (DIL_STEP * DIL_STEP)

        def put(ref, r, c, val):
            col = (r * n_tiles + c) * LANES
            ref[:, col:col + LANES] = val.astype(BF16)

        for c in range(n_tiles):
            put(out_refs[0], 0, c, y[:, c * LANES:(c + 1) * LANES])
            stage_a[c] = y[:, c * LANES:(c + 1) * LANES]
        for r4 in range(DIL_STEP):
            for c in range(n_tiles):
                cls = stage_a[c, pl.ds(r4, rows4, stride=DIL_STEP), :]
                put(out_refs[1], r4, c, cls)
                stage_b[c, r4 * rows4:(r4 + 1) * rows4, :] = cls
        for r4 in range(DIL_STEP):
            for q in range(DIL_STEP):
                for c in range(n_tiles):
                    put(out_refs[2], r4 + DIL_STEP * q, c,
                        stage_b[c, pl.ds(r4 * rows4 + q, rows16, stride=DIL_STEP), :])

    wa, wb = WIDTH_A, WIDTH_B
    p_a = project(0, 2 * wa)
    p_b = project(2 * wa, 2 * wb)
    qk_a = rope(normed(p_a, gqka_ref))
    emit_classes(qk_a[:, :wa], qa_refs, stages[0], stages[1])
    emit_classes(qk_a[:, wa:], ka_refs, stages[2], stages[3])
    p_v = project(2 * wa + 2 * wb, wa + wb)
    qk_b = normed(p_b, gqkb_ref).astype(BF16)
    qb_ref[...] = qk_b[:, :wb]
    kb_ref[...] = qk_b[:, wb:]
    p_m = project(3 * wa + 3 * wb, WIDTH_M)
    emit_classes(p_v[:, :wa], va_refs, stages[4], stages[5])
    vb_ref[...] = p_v[:, wa:].astype(BF16)
    qm_ref[...] = normed(p_m, gqm_ref).astype(BF16)


def _proj(x, g, w_in, bd, gqka, gqkb, gqm, cos_t, sin_t):
    b, t, _ = x.shape
    tm = PROJ_TM
    tok = lambda width: pl.BlockSpec((None, tm, width), lambda bi, i: (bi, i, 0))
    cls = lambda dil, w: pl.BlockSpec((None, tm // dil, dil * w), lambda bi, i: (bi, i, 0))
    cls_shape = lambda dil, w: jax.ShapeDtypeStruct((b, t // dil, dil * w), BF16)
    cls_widths = (WIDTH_A, WIDTH_A, WIDTH_A)
    rest_widths = (WIDTH_B, WIDTH_B, WIDTH_B, WIDTH_M)
    n = len(DILATIONS)
    outs = pl.pallas_call(
        _proj_kernel,
        grid=(b, t // tm),
        in_specs=[tok(D_MODEL), _const_spec((1, D_MODEL)), _const_spec((D_MODEL, IN_WIDTH)),
                  _const_spec((2 * LANES, 2 * LANES)),
                  _const_spec((1, 2 * WIDTH_A)), _const_spec((1, 2 * WIDTH_B)), _const_spec((1, WIDTH_M)),
                  pl.BlockSpec((tm, LANES), lambda bi, i: (i, 0)),
                  pl.BlockSpec((tm, LANES), lambda bi, i: (i, 0))],
        out_specs=[cls(d, w) for w in cls_widths for d in DILATIONS] + [tok(w) for w in rest_widths],
        out_shape=[cls_shape(d, w) for w in cls_widths for d in DILATIONS]
                  + [jax.ShapeDtypeStruct((b, t, w), BF16) for w in rest_widths],
        scratch_shapes=[pltpu.VMEM((WIDTH_A // LANES, tm, LANES), F32)] * 6,
        compiler_params=_cparams(2),
        name="proj",
    )(x, g, w_in, bd, gqka, gqkb, gqm, cos_t, sin_t)
    return outs[:n], outs[n:2 * n], outs[2 * n:3 * n], outs[3 * n:]


def _memkv_kernel(mem_ref, g_ref, w_ref, bd_ref, gk_ref, km_ref, vaug_ref):
    mn = _rms(mem_ref[...], g_ref[...]).astype(BF16)
    kv = jnp.dot(mn, w_ref[...], preferred_element_type=F32)
    km_ref[...] = _head_rms(kv[:, :WIDTH_M], bd_ref[...], gk_ref[...]).astype(BF16)
    _fill_vaug(vaug_ref, 0, kv[:, WIDTH_M:].astype(BF16))


def _memkv(mem, g, w_kv, bd, gk):
    b = mem.shape[0]
    return pl.pallas_call(
        _memkv_kernel,
        grid=(b,),
        in_specs=[pl.BlockSpec((None, N_MEM, D_MODEL), lambda bi: (bi, 0, 0)),
                  _const_spec((1, D_MODEL)), _const_spec((D_MODEL, 2 * WIDTH_M)),
                  _const_spec((2 * LANES, 2 * LANES)), _const_spec((1, WIDTH_M))],
        out_specs=[pl.BlockSpec((None, N_MEM, WIDTH_M), lambda bi: (bi, 0, 0)),
                   pl.BlockSpec((None, N_MEM, 2 * WIDTH_M), lambda bi: (bi, 0, 0))],
        out_shape=[jax.ShapeDtypeStruct((b, N_MEM, WIDTH_M), BF16),
                   jax.ShapeDtypeStruct((b, N_MEM, 2 * WIDTH_M), BF16)],
        compiler_params=_cparams(1),
        name="memkv",
    )(mem, g, w_kv, bd, gk)


DIL_QB = 2048
DIL_SUB = 2 * HALF


def _dilated_kernel(q_ref, kp_ref, kc_ref, kn_ref, vp_ref, vc_ref, vn_ref, bias_ref,
                    o_ref, lse_ref, kbuf, vaug, *, n_sub_total):
    qb = q_ref.shape[0]
    kbuf[0:HALF] = kp_ref[...]
    kbuf[HALF:HALF + qb] = kc_ref[...]
    kbuf[HALF + qb:2 * HALF + qb] = kn_ref[...]
    _fill_vaug(vaug, 0, vp_ref[...])
    _fill_vaug(vaug, HALF, vc_ref[...])
    _fill_vaug(vaug, HALF + qb, vn_ref[...])
    n_sub = qb // DIL_SUB
    first_sub = pl.program_id(2) * n_sub

    for j in range(n_sub):
        gs = first_sub + j
        variant = jnp.where(gs == 0, 0, jnp.where(gs == n_sub_total - 1, 2, 1))
        r0 = j * DIL_SUB
        for hp in range(q_ref.shape[1] // LANES):
            o, lse = _attend_pair(q_ref[r0:r0 + DIL_SUB, LANES * hp:LANES * (hp + 1)],
                                  kbuf[r0:r0 + 2 * DIL_SUB, LANES * hp:LANES * (hp + 1)],
                                  vaug[r0:r0 + 2 * DIL_SUB, 2 * LANES * hp:2 * LANES * (hp + 1)],
                                  bias_ref[variant])
            o_ref[r0:r0 + DIL_SUB, LANES * hp:LANES * (hp + 1)] = o.astype(BF16)
            lse_ref[r0:r0 + DIL_SUB, LANES * hp:LANES * (hp + 1)] = lse


def _dilated(q, k, v, bias, dil):
    b, ln = q.shape[0], q.shape[1]
    qb = min(DIL_QB, ln)
    classes = min(dil, DIL_QB // qb)
    w = classes * (q.shape[2] // dil)
    hb = qb // HALF
    n_halo = ln // HALF
    main = pl.BlockSpec((None, qb, w), lambda bi, r, i: (bi, i, r))
    prev = pl.BlockSpec((None, HALF, w), lambda bi, r, i: (bi, jnp.maximum(i * hb - 1, 0), r))
    nxt = pl.BlockSpec((None, HALF, w), lambda bi, r, i: (bi, jnp.minimum((i + 1) * hb, n_halo - 1), r))
    o, lse = pl.pallas_call(
        functools.partial(_dilated_kernel, n_sub_total=ln // DIL_SUB),
        grid=(b, dil // classes, ln // qb),
        in_specs=[main, prev, main, nxt, prev, main, nxt, _const_spec(bias.shape)],
        out_specs=[main, main],
        out_shape=[jax.ShapeDtypeStruct(q.shape, BF16), jax.ShapeDtypeStruct(q.shape, F32)],
        scratch_shapes=[pltpu.VMEM((qb + 2 * HALF, w), BF16),
                        pltpu.VMEM((qb + 2 * HALF, 2 * w), BF16)],
        compiler_params=_cparams(3),
        name=f"dilated{dil}",
    )(q, k, k, k, v, v, v, bias)
    return o, lse


def _band_bias():
    row = np.arange(DIL_SUB)[:, None]
    col = np.arange(2 * DIL_SUB)[None, :]
    band = (col - row >= 0) & (col - row <= 2 * HALF)
    variants = [band & (col >= HALF), band, band & (col < 2 * DIL_SUB - HALF)]
    tab = np.stack([np.where(np.concatenate([m, m], axis=0), 0.0, NEG) for m in variants])
    return jnp.asarray(tab, dtype=F32)


NA_GROUP = 8
NA_TOK = NA_GROUP * GRID_W
NA_KEYS = NA_ROWS * GRID_W


def _natten_kernel(q_ref, kp_ref, kc_ref, kn_ref, vp_ref, vc_ref, vn_ref, bias_ref, qm_ref, km_ref, vm_ref,
                   o_ref, om_ref, kbuf, vaug, *, n_rows):
    kbuf[0:NA_TOK] = kp_ref[...]
    kbuf[NA_TOK:2 * NA_TOK] = kc_ref[...]
    kbuf[2 * NA_TOK:3 * NA_TOK] = kn_ref[...]
    pl.when(_first_step(2))(lambda: _fill_ones(vaug))
    _fill_vaug(vaug, 0, vp_ref[...], ones=False)
    _fill_vaug(vaug, NA_TOK, vc_ref[...], ones=False)
    _fill_vaug(vaug, 2 * NA_TOK, vn_ref[...], ones=False)
    g = pl.program_id(1)

    for j in range(NA_GROUP):
        r = g * NA_GROUP + j
        r0 = jnp.clip(r - NA_ROWS // 2, 0, n_rows - NA_ROWS)
        off = r - r0
        start = pl.multiple_of((r0 - (g - 1) * NA_GROUP) * GRID_W, GRID_W)
        q0 = j * GRID_W
        for hp in range(WIDTH_B // LANES):
            o, _ = _attend_pair(q_ref[q0:q0 + GRID_W, LANES * hp:LANES * (hp + 1)],
                                kbuf[pl.ds(start, NA_KEYS), LANES * hp:LANES * (hp + 1)],
                                vaug[pl.ds(start, NA_KEYS), 2 * LANES * hp:2 * LANES * (hp + 1)],
                                bias_ref[off, hp])
            o_ref[q0:q0 + GRID_W, LANES * hp:LANES * (hp + 1)] = o.astype(BF16)
        if j % 2 == 1:
            m0 = (j - 1) * GRID_W
            for hp in range(WIDTH_M // LANES):
                o, _ = _attend_pair(qm_ref[m0:m0 + 2 * GRID_W, LANES * hp:LANES * (hp + 1)],
                                    km_ref[:, LANES * hp:LANES * (hp + 1)],
                                    vm_ref[:, 2 * LANES * hp:2 * LANES * (hp + 1)], None)
                om_ref[m0:m0 + 2 * GRID_W, LANES * hp:LANES * (hp + 1)] = o.astype(BF16)


def _natten(q, k, v, bias, qm, km, vm):
    b, t, w = q.shape
    n_rows = t // GRID_W
    n_groups = n_rows // NA_GROUP
    blk = lambda f: pl.BlockSpec((None, NA_TOK, w), lambda bi, g: (bi, f(g), 0))
    cur = blk(lambda g: g)
    prev = blk(lambda g: jnp.maximum(g - 1, 0))
    nxt = blk(lambda g: jnp.minimum(g + 1, n_groups - 1))
    cur_m = pl.BlockSpec((None, NA_TOK, WIDTH_M), lambda bi, g: (bi, g, 0))
    per_b = lambda width: pl.BlockSpec((None, N_MEM, width), lambda bi, g: (bi, 0, 0))
    return pl.pallas_call(
        functools.partial(_natten_kernel, n_rows=n_rows),
        grid=(b, n_groups),
        in_specs=[cur, prev, cur, nxt, prev, cur, nxt, _const_spec(bias.shape),
                  cur_m, per_b(WIDTH_M), per_b(2 * WIDTH_M)],
        out_specs=[cur, cur_m],
        out_shape=[jax.ShapeDtypeStruct((b, t, w), BF16), jax.ShapeDtypeStruct((b, t, WIDTH_M), BF16)],
        scratch_shapes=[pltpu.VMEM((3 * NA_TOK, w), BF16), pltpu.VMEM((3 * NA_TOK, 2 * w), BF16)],
        compiler_params=_cparams(2, in_order=True),
        name="natten",
    )(q, k, k, k, v, v, v, bias, qm, km, vm)


def _natten_bias(rpb):
    c = np.arange(GRID_W)[:, None]
    kc = np.arange(GRID_W)[None, :]
    c0 = np.clip(c - NA_COLS // 2, 0, GRID_W - NA_COLS)
    valid = (kc >= c0) & (kc < c0 + NA_COLS)
    dcol = kc - c + NA_COLS - 1
    col_sel = (valid[..., None] & (dcol[..., None] == np.arange(2 * NA_COLS - 1))).astype(np.float32)
    cols = jnp.einsum("hrd,ckd->hcrk", rpb.astype(F32), col_sel, precision=lax.Precision.HIGHEST)
    cols = jnp.where(valid[None, :, None, :], cols * LOG2E, NEG)
    cols = cols.reshape(N_HEADS_B // 2, 2 * GRID_W, (2 * NA_ROWS - 1) * GRID_W)
    tab = jnp.stack([cols[:, :, (NA_ROWS - 1 - off) * GRID_W:(2 * NA_ROWS - 1 - off) * GRID_W]
                     for off in range(NA_ROWS)])
    return tab


FIN_TM = 512
FF_CHUNK = 1024


def _final_kernel(x_ref, o1_ref, o4_ref, o16_ref, l1_ref, l4_ref, l16_ref, ob_ref, om_ref,
                  ga_ref, gb_ref, gm_ref, wo_ref, gf_ref, w1_ref, w2_ref, y_ref,
                  *stages):
    ts = x_ref.shape[0]
    base = 0
    n_tiles = WIDTH_A // LANES
    rows4, rows16 = ts // DIL_STEP, ts // (DIL_STEP * DIL_STEP)

    def token_order(ref, base, dil, stage, stage_b=None):
        src = lambda col: ref[base // dil:(base + ts) // dil, col:col + LANES].astype(F32)
        if stage_b is not None:
            for r4 in range(DIL_STEP):
                for q in range(DIL_STEP):
                    for c in range(n_tiles):
                        stage_b[c, pl.ds(base + r4 * rows4 + q, rows16, stride=DIL_STEP), :] = (
                            src((r4 + DIL_STEP * q) * WIDTH_A + c * LANES))
        for r4 in range(DIL_STEP):
            for c in range(n_tiles):
                if stage_b is not None:
                    cls = stage_b[c, base + r4 * rows4:base + (r4 + 1) * rows4, :]
                else:
                    cls = src(r4 * WIDTH_A + c * LANES)
                stage[c, pl.ds(base + r4, rows4, stride=DIL_STEP), :] = cls
        return jnp.concatenate([stage[c, base:base + ts, :] for c in range(n_tiles)], axis=1)

    o1, l1 = o1_ref[...].astype(F32), l1_ref[...]
    o4 = token_order(o4_ref, base, DILATIONS[1], stages[0])
    l4 = token_order(l4_ref, base, DILATIONS[1], stages[1])
    o16 = token_order(o16_ref, base, DILATIONS[2], stages[2], stages[4])
    l16 = token_order(l16_ref, base, DILATIONS[2], stages[3], stages[5])
    mx = jnp.maximum(jnp.maximum(l1, l4), l16)
    e1, e4, e16 = jnp.exp(l1 - mx), jnp.exp(l4 - mx), jnp.exp(l16 - mx)
    oa = (e1 * o1 + e4 * o4 + e16 * o16) / (e1 + e4 + e16)
    mixed = jnp.concatenate([_rms(oa, ga_ref[...]), _rms(ob_ref[...].astype(F32), gb_ref[...]),
                             _rms(om_ref[...].astype(F32), gm_ref[...])], axis=1).astype(BF16)
    x1 = x_ref[...] + jnp.dot(mixed, wo_ref[...], preferred_element_type=F32)
    hf = _rms(x1, gf_ref[...]).astype(BF16)
    acc = x1
    for c in range(0, D_FF, FF_CHUNK):
        h = jnp.dot(hf, w1_ref[:, c:c + FF_CHUNK], preferred_element_type=F32)
        h = jnp.square(jnp.maximum(h, 0.0)).astype(BF16)
        acc = acc + jnp.dot(h, w2_ref[c:c + FF_CHUNK, :], preferred_element_type=F32)
    y_ref[...] = acc


def _final(x, o1, o4, o16, l1, l4, l16, ob, om, ga, gb, gm, wo, gf, w1, w2):
    b, t, _ = x.shape
    tm = FIN_TM
    tok = lambda width: pl.BlockSpec((None, tm, width), lambda bi, i: (bi, i, 0))
    cls = [pl.BlockSpec((None, tm // d, d * WIDTH_A), lambda bi, i: (bi, i, 0)) for d in DILATIONS]
    return pl.pallas_call(
        _final_kernel,
        grid=(b, t // tm),
        in_specs=[tok(D_MODEL), *cls, *cls, tok(WIDTH_B), tok(WIDTH_M),
                  _const_spec((1, WIDTH_A)), _const_spec((1, WIDTH_B)), _const_spec((1, WIDTH_M)),
                  _const_spec((D_MODEL, D_MODEL)), _const_spec((1, D_MODEL)),
                  _const_spec((D_MODEL, D_FF)), _const_spec((D_FF, D_MODEL))],
        out_specs=tok(D_MODEL),
        out_shape=jax.ShapeDtypeStruct((b, t, D_MODEL), F32),
        scratch_shapes=[pltpu.VMEM((WIDTH_A // LANES, tm, LANES), F32)] * 6,
        compiler_params=_cparams(2),
        name="final",
    )(x, o1, o4, o16, l1, l4, l16, ob, om, ga, gb, gm, wo, gf, w1, w2)


def _rope_tables(t):
    half = ROPE_DIMS // 2
    inv = ROPE_THETA ** (-(np.arange(half, dtype=np.float64) * 2.0 / ROPE_DIMS))
    ang = np.arange(t, dtype=np.float64)[:, None] * inv[None, :]
    cs = jnp.asarray(np.concatenate([np.cos(ang), np.sin(ang)], axis=1), dtype=F32)
    cos, sin = cs[:, :half], cs[:, half:]
    rest = HEAD_DIM - ROPE_DIMS
    cos_h = jnp.concatenate([cos, cos, jnp.ones((t, rest), F32)], axis=1)
    sin_h = jnp.concatenate([-sin, sin, jnp.zeros((t, rest), F32)], axis=1)
    return jnp.tile(cos_h, (1, LANES // HEAD_DIM)), jnp.tile(sin_h, (1, LANES // HEAD_DIM))


def _group_columns(w):
    a, b = WIDTH_A, WIDTH_B
    qa, ka, va = w[:, :a], w[:, a:2 * a], w[:, 2 * a:3 * a]
    qb, kb, vb = w[:, 3 * a:3 * a + b], w[:, 3 * a + b:3 * a + 2 * b], w[:, 3 * a + 2 * b:3 * a + 3 * b]
    return jnp.concatenate([qa, ka, qb, kb, va, vb, w[:, 3 * a + 3 * b:]], axis=1)


def _layer(x, mem, p, tables):
    cos_t, sin_t = tables[x.shape[1]]
    qa, ka, va, (qb, kb, vb, qm) = _proj(x, p["g_attn"], p["w_in"], p["bd"], p["gqka"], p["gqkb"],
                                         p["gqm"], cos_t, sin_t)
    km, vm = _memkv(mem, p["g_mem"], p["w_kv"], p["bd"], p["gkm"])
    pats = [_dilated(qa[n], ka[n], va[n], p["band"], d) for n, d in enumerate(DILATIONS)]
    ob, om = _natten(qb, kb, vb, p["na_bias"], qm, km, vm)
    return _final(x, pats[0][0], pats[1][0], pats[2][0], pats[0][1], pats[1][1], pats[2][1],
                  ob, om, p["ga"], p["gb"], p["gm"], p["w_out"], p["g_ffn"], p["w1"], p["w2"])


def kernel(x_prompt, x_sample, mem_prompt, mem_sample, norm_attn, w_in, q_norm_a, k_norm_a, q_norm_b, k_norm_b, rpb_b, norm_mem, w_mem_kv, q_norm_m, k_norm_m, out_norm_a, out_norm_b, out_norm_m, w_out, norm_ffn, w_ff1, w_ff2):
    depth = w_in.shape[0]
    scale = HEAD_DIM ** -0.5 * LOG2E
    row = lambda v: v.astype(F32)[None, :]
    heads = lambda v, n: jnp.tile(v.astype(F32), n)[None, :]
    idx = np.arange(2 * LANES)
    bd = jnp.asarray(idx[:, None] // HEAD_DIM == idx[None, :] // HEAD_DIM, dtype=BF16)
    band = _band_bias()
    tables = {t: _rope_tables(t) for t in {x_prompt.shape[1], x_sample.shape[1]}}
    y_prompt, y_sample = x_prompt, x_sample
    for i in range(depth):
        p = dict(
            g_attn=row(norm_attn[i]), w_in=_group_columns(w_in[i]).astype(BF16), bd=bd, band=band,
            gqka=jnp.concatenate([heads(q_norm_a[i], N_HEADS_A) * scale, heads(k_norm_a[i], N_HEADS_A)], axis=1),
            gqkb=jnp.concatenate([heads(q_norm_b[i], N_HEADS_B) * scale, heads(k_norm_b[i], N_HEADS_B)], axis=1),
            gqm=heads(q_norm_m[i], N_HEADS_M) * scale, gkm=heads(k_norm_m[i], N_HEADS_M),
            na_bias=_natten_bias(rpb_b[i]),
            g_mem=row(norm_mem[i]), w_kv=w_mem_kv[i].astype(BF16),
            ga=row(out_norm_a[i]), gb=row(out_norm_b[i]), gm=row(out_norm_m[i]),
            w_out=w_out[i].astype(BF16), g_ffn=row(norm_ffn[i]),
            w1=w_ff1[i].astype(BF16), w2=w_ff2[i].astype(BF16))
        y_prompt = _layer(y_prompt, mem_prompt, p, tables)
        y_sample = _layer(y_sample, mem_sample, p, tables)
    return (y_prompt, y_sample)
```

```python
import functools

import numpy as np
import jax
import jax.numpy as jnp
from jax import lax
from jax.experimental import pallas as pl
from jax.experimental.pallas import tpu as pltpu

D_MODEL = 1024
HEAD_DIM = 64
N_HEADS_A = 6
N_HEADS_B = 6
N_HEADS_M = 4
WIDTH_A = N_HEADS_A * HEAD_DIM
WIDTH_B = N_HEADS_B * HEAD_DIM
WIDTH_M = N_HEADS_M * HEAD_DIM
IN_WIDTH = 3 * WIDTH_A + 3 * WIDTH_B + WIDTH_M
DILATIONS = (1, 4, 16)
DIL_STEP = 4
HALF = 64
GRID_W = 64
NA_ROWS = 8
NA_COLS = 16
N_MEM = 256
D_FF = 4 * D_MODEL
ROPE_THETA = 500000.0
ROPE_DIMS = HEAD_DIM // 4
EPS = 1e-6
NEG = -1e30
LOG2E = 1.4426950408889634
LN2 = 0.6931471805599453

LANES = 128
PAIR = 2 * HEAD_DIM
VMEM_LIMIT = 56 * 1024 * 1024

BF16 = jnp.bfloat16
F32 = jnp.float32


def _cparams(n_axes, in_order=False):
    return pltpu.CompilerParams(dimension_semantics=("arbitrary" if in_order else "parallel",) * n_axes,
                                vmem_limit_bytes=VMEM_LIMIT)


def _const_spec(shape):
    nd = len(shape)
    return pl.BlockSpec(shape, lambda *_: (0,) * nd, pipeline_mode=pl.Buffered(1))


def _rms(x, gain):
    return x * lax.rsqrt(jnp.mean(x * x, axis=-1, keepdims=True) + EPS) * gain


def _head_rms(z, bd, gain):
    ss = jnp.dot((z * z).astype(BF16), bd, preferred_element_type=F32)
    return z * lax.rsqrt(ss * (1.0 / HEAD_DIM) + EPS) * gain


def _attend_pair(qp, kp, vaug, bias):
    p, mx = _pair_probs(_pair_scores(qp, kp, bias))
    return _pair_output(p, mx, vaug)


def _pair_scores(qp, kp, bias):
    lane = lax.broadcasted_iota(jnp.int32, qp.shape, 1)
    zero = jnp.zeros_like(qp)
    lhs = jnp.concatenate([jnp.where(lane < HEAD_DIM, qp, zero),
                           jnp.where(lane >= HEAD_DIM, qp, zero)], axis=0)
    s = lax.dot_general(lhs, kp, (((1,), (1,)), ((), ())), preferred_element_type=F32)
    return s if bias is None else s + bias


def _pair_probs(s):
    mx = jnp.max(s, axis=-1, keepdims=True)
    return jnp.exp2(s - mx).astype(BF16), mx


def _pair_output(p, mx, vaug):
    m_rows = p.shape[0] // 2
    ov = jnp.dot(p, vaug, preferred_element_type=F32)
    first = lax.broadcasted_iota(jnp.int32, (m_rows, LANES), 1) < HEAD_DIM
    o = jnp.where(first, ov[:m_rows, :LANES], ov[m_rows:, :LANES])
    l = jnp.where(first, ov[:m_rows, LANES:], ov[m_rows:, LANES:])
    mm = jnp.where(first, mx[:m_rows], mx[m_rows:])
    return o / l, mm * LN2 + jnp.log(l)


def _fill_vaug(vaug_ref, row0, v, ones=True):
    rows = v.shape[0]
    for hp in range(v.shape[1] // LANES):
        vaug_ref[row0:row0 + rows, 2 * LANES * hp:2 * LANES * hp + LANES] = v[:, LANES * hp:LANES * (hp + 1)]
        if ones:
            vaug_ref[row0:row0 + rows, 2 * LANES * hp + LANES:2 * LANES * (hp + 1)] = jnp.ones((rows, LANES), BF16)


def _fill_ones(vaug_ref):
    for hp in range(vaug_ref.shape[1] // (2 * LANES)):
        vaug_ref[:, 2 * LANES * hp + LANES:2 * LANES * (hp + 1)] = jnp.ones((vaug_ref.shape[0], LANES), BF16)


def _first_step(n_axes):
    first = pl.program_id(0) == 0
    for axis in range(1, n_axes):
        first = first & (pl.program_id(axis) == 0)
    return first


PROJ_TM = 1024


def _proj_kernel(x_ref, g_ref, w_ref, bd_ref, gqka_ref, gqkb_ref, gqm_ref, cos_ref, sin_ref, *refs):
    n_dil = len(DILATIONS)
    qa_refs, ka_refs, va_refs = refs[:n_dil], refs[n_dil:2 * n_dil], refs[2 * n_dil:3 * n_dil]
    qb_ref, kb_ref, vb_ref, qm_ref = refs[3 * n_dil:3 * n_dil + 4]
    stages = refs[3 * n_dil + 4:]
    tm = x_ref.shape[0]
    x = x_ref[...]
    xn = _rms(x, g_ref[...]).astype(BF16)
    bd = bd_ref[...]

    def project(c0, width):
        return jnp.dot(xn, w_ref[:, c0:c0 + width], preferred_element_type=F32)

    def normed(z, gain_ref):
        return jnp.concatenate(
            [_head_rms(z[:, off:off + 2 * LANES], bd, gain_ref[:, off:off + 2 * LANES])
             for off in range(0, z.shape[1], 2 * LANES)], axis=1)

    def rope(y):
        cos = cos_ref[...]
        sin = sin_ref[...]
        lane = lax.broadcasted_iota(jnp.int32, cos.shape, 1) % HEAD_DIM
        low = lane < ROPE_DIMS // 2
        parts = []
        for c in range(0, y.shape[1], LANES):
            yc = y[:, c:c + LANES]
            partner = jnp.where(low, pltpu.roll(yc, LANES - ROPE_DIMS // 2, axis=1),
                                pltpu.roll(yc, ROPE_DIMS // 2, axis=1))
            parts.append(yc * cos + partner * sin)
        return jnp.concatenate(parts, axis=1)

    def emit_classes(y, out_refs, stage_a, stage_b):
        n_tiles = WIDTH_A // LANES
        rows4, rows16 = tm // DIL_STEP, tm // (DIL_STEP * DIL_STEP)

        def put(ref, r, c, val):
            col = (r * n_tiles + c) * LANES
            ref[:, col:col + LANES] = val.astype(BF16)

        for c in range(n_tiles):
            put(out_refs[0], 0, c, y[:, c * LANES:(c + 1) * LANES])
            stage_a[c] = y[:, c * LANES:(c + 1) * LANES]
        for r4 in range(DIL_STEP):
            for c in range(n_tiles):
                cls = stage_a[c, pl.ds(r4, rows4, stride=DIL_STEP), :]
                put(out_refs[1], r4, c, cls)
                stage_b[c, r4 * rows4:(r4 + 1) * rows4, :] = cls
        for r4 in range(DIL_STEP):
            for q in range(DIL_STEP):
                for c in range(n_tiles):
                    put(out_refs[2], r4 + DIL_STEP * q, c,
                        stage_b[c, pl.ds(r4 * rows4 + q, rows16, stride=DIL_STEP), :])

    wa, wb = WIDTH_A, WIDTH_B
    p_a = project(0, 2 * wa)
    p_b = project(2 * wa, 2 * wb)
    qk_a = rope(normed(p_a, gqka_ref))
    emit_classes(qk_a[:, :wa], qa_refs, stages[0], stages[1])
    emit_classes(qk_a[:, wa:], ka_refs, stages[2], stages[3])
    p_v = project(2 * wa + 2 * wb, wa + wb)
    qk_b = normed(p_b, gqkb_ref).astype(BF16)
    qb_ref[...] = qk_b[:, :wb]
    kb_ref[...] = qk_b[:, wb:]
    p_m = project(3 * wa + 3 * wb, WIDTH_M)
    emit_classes(p_v[:, :wa], va_refs, stages[4], stages[5])
    vb_ref[...] = p_v[:, wa:].astype(BF16)
    qm_ref[...] = normed(p_m, gqm_ref).astype(BF16)


def _proj(x, g, w_in, bd, gqka, gqkb, gqm, cos_t, sin_t):
    b, t, _ = x.shape
    tm = PROJ_TM
    tok = lambda width: pl.BlockSpec((None, tm, width), lambda bi, i: (bi, i, 0))
    cls = lambda dil, w: pl.BlockSpec((None, tm // dil, dil * w), lambda bi, i: (bi, i, 0))
    cls_shape = lambda dil, w: jax.ShapeDtypeStruct((b, t // dil, dil * w), BF16)
    cls_widths = (WIDTH_A, WIDTH_A, WIDTH_A)
    rest_widths = (WIDTH_B, WIDTH_B, WIDTH_B, WIDTH_M)
    n = len(DILATIONS)
    outs = pl.pallas_call(
        _proj_kernel,
        grid=(b, t // tm),
        in_specs=[tok(D_MODEL), _const_spec((1, D_MODEL)), _const_spec((D_MODEL, IN_WIDTH)),
                  _const_spec((2 * LANES, 2 * LANES)),
                  _const_spec((1, 2 * WIDTH_A)), _const_spec((1, 2 * WIDTH_B)), _const_spec((1, WIDTH_M)),
                  pl.BlockSpec((tm, LANES), lambda bi, i: (i, 0)),
                  pl.BlockSpec((tm, LANES), lambda bi, i: (i, 0))],
        out_specs=[cls(d, w) for w in cls_widths for d in DILATIONS] + [tok(w) for w in rest_widths],
        out_shape=[cls_shape(d, w) for w in cls_widths for d in DILATIONS]
                  + [jax.ShapeDtypeStruct((b, t, w), BF16) for w in rest_widths],
        scratch_shapes=[pltpu.VMEM((WIDTH_A // LANES, tm, LANES), F32)] * 6,
        compiler_params=_cparams(2),
        name="proj",
    )(x, g, w_in, bd, gqka, gqkb, gqm, cos_t, sin_t)
    return outs[:n], outs[n:2 * n], outs[2 * n:3 * n], outs[3 * n:]


def _memkv_kernel(mem_ref, g_ref, w_ref, bd_ref, gk_ref, km_ref, vaug_ref):
    mn = _rms(mem_ref[...], g_ref[...]).astype(BF16)
    kv = jnp.dot(mn, w_ref[...], preferred_element_type=F32)
    km_ref[...] = _head_rms(kv[:, :WIDTH_M], bd_ref[...], gk_ref[...]).astype(BF16)
    _fill_vaug(vaug_ref, 0, kv[:, WIDTH_M:].astype(BF16))


def _memkv(mem, g, w_kv, bd, gk):
    b = mem.shape[0]
    return pl.pallas_call(
        _memkv_kernel,
        grid=(b,),
        in_specs=[pl.BlockSpec((None, N_MEM, D_MODEL), lambda bi: (bi, 0, 0)),
                  _const_spec((1, D_MODEL)), _const_spec((D_MODEL, 2 * WIDTH_M)),
                  _const_spec((2 * LANES, 2 * LANES)), _const_spec((1, WIDTH_M))],
        out_specs=[pl.BlockSpec((None, N_MEM, WIDTH_M), lambda bi: (bi, 0, 0)),
                   pl.BlockSpec((None, N_MEM, 2 * WIDTH_M), lambda bi: (bi, 0, 0))],
        out_shape=[jax.ShapeDtypeStruct((b, N_MEM, WIDTH_M), BF16),
                   jax.ShapeDtypeStruct((b, N_MEM, 2 * WIDTH_M), BF16)],
        compiler_params=_cparams(1),
        name="memkv",
    )(mem, g, w_kv, bd, gk)


DIL_QB = 4096
DIL_SUB = 2 * HALF


def _dilated_kernel(q_ref, kp_ref, kc_ref, kn_ref, vp_ref, vc_ref, vn_ref, bias_ref,
                    o_ref, lse_ref, kbuf, vaug, *, n_sub_total):
    qb = q_ref.shape[0]
    kbuf[0:HALF] = kp_ref[...]
    kbuf[HALF:HALF + qb] = kc_ref[...]
    kbuf[HALF + qb:2 * HALF + qb] = kn_ref[...]
    _fill_vaug(vaug, 0, vp_ref[...])
    _fill_vaug(vaug, HALF, vc_ref[...])
    _fill_vaug(vaug, HALF + qb, vn_ref[...])
    n_sub = qb // DIL_SUB
    first_sub = pl.program_id(2) * n_sub

    for j in range(n_sub):
        gs = first_sub + j
        variant = jnp.where(gs == 0, 0, jnp.where(gs == n_sub_total - 1, 2, 1))
        r0 = j * DIL_SUB
        for hp in range(q_ref.shape[1] // LANES):
            o, lse = _attend_pair(q_ref[r0:r0 + DIL_SUB, LANES * hp:LANES * (hp + 1)],
                                  kbuf[r0:r0 + 2 * DIL_SUB, LANES * hp:LANES * (hp + 1)],
                                  vaug[r0:r0 + 2 * DIL_SUB, 2 * LANES * hp:2 * LANES * (hp + 1)],
                                  bias_ref[variant])
            o_ref[r0:r0 + DIL_SUB, LANES * hp:LANES * (hp + 1)] = o.astype(BF16)
            lse_ref[r0:r0 + DIL_SUB, LANES * hp:LANES * (hp + 1)] = lse


def _dilated(q, k, v, bias, dil):
    b, ln = q.shape[0], q.shape[1]
    qb = min(DIL_QB, ln)
    classes = min(dil, DIL_QB // qb)
    w = classes * (q.shape[2] // dil)
    hb = qb // HALF
    n_halo = ln // HALF
    main = pl.BlockSpec((None, qb, w), lambda bi, r, i: (bi, i, r))
    prev = pl.BlockSpec((None, HALF, w), lambda bi, r, i: (bi, jnp.maximum(i * hb - 1, 0), r))
    nxt = pl.BlockSpec((None, HALF, w), lambda bi, r, i: (bi, jnp.minimum((i + 1) * hb, n_halo - 1), r))
    o, lse = pl.pallas_call(
        functools.partial(_dilated_kernel, n_sub_total=ln // DIL_SUB),
        grid=(b, dil // classes, ln // qb),
        in_specs=[main, prev, main, nxt, prev, main, nxt, _const_spec(bias.shape)],
        out_specs=[main, main],
        out_shape=[jax.ShapeDtypeStruct(q.shape, BF16), jax.ShapeDtypeStruct(q.shape, F32)],
        scratch_shapes=[pltpu.VMEM((qb + 2 * HALF, w), BF16),
                        pltpu.VMEM((qb + 2 * HALF, 2 * w), BF16)],
        compiler_params=_cparams(3),
        name=f"dilated{dil}",
    )(q, k, k, k, v, v, v, bias)
    return o, lse


def _band_bias():
    row = np.arange(DIL_SUB)[:, None]
    col = np.arange(2 * DIL_SUB)[None, :]
    band = (col - row >= 0) & (col - row <= 2 * HALF)
    variants = [band & (col >= HALF), band, band & (col < 2 * DIL_SUB - HALF)]
    tab = np.stack([np.where(np.concatenate([m, m], axis=0), 0.0, NEG) for m in variants])
    return jnp.asarray(tab, dtype=F32)


NA_GROUP = 16
NA_TOK = NA_GROUP * GRID_W
NA_KEYS = NA_ROWS * GRID_W


def _natten_kernel(q_ref, kp_ref, kc_ref, kn_ref, vp_ref, vc_ref, vn_ref, bias_ref, qm_ref, km_ref, vm_ref,
                   o_ref, om_ref, kbuf, vaug, *, n_rows):
    kbuf[0:NA_TOK] = kp_ref[...]
    kbuf[NA_TOK:2 * NA_TOK] = kc_ref[...]
    kbuf[2 * NA_TOK:3 * NA_TOK] = kn_ref[...]
    pl.when(_first_step(2))(lambda: _fill_ones(vaug))
    _fill_vaug(vaug, 0, vp_ref[...], ones=False)
    _fill_vaug(vaug, NA_TOK, vc_ref[...], ones=False)
    _fill_vaug(vaug, 2 * NA_TOK, vn_ref[...], ones=False)
    g = pl.program_id(1)

    for j in range(NA_GROUP):
        r = g * NA_GROUP + j
        r0 = jnp.clip(r - NA_ROWS // 2, 0, n_rows - NA_ROWS)
        off = r - r0
        start = pl.multiple_of((r0 - (g - 1) * NA_GROUP) * GRID_W, GRID_W)
        q0 = j * GRID_W
        for hp in range(WIDTH_B // LANES):
            o, _ = _attend_pair(q_ref[q0:q0 + GRID_W, LANES * hp:LANES * (hp + 1)],
                                kbuf[pl.ds(start, NA_KEYS), LANES * hp:LANES * (hp + 1)],
                                vaug[pl.ds(start, NA_KEYS), 2 * LANES * hp:2 * LANES * (hp + 1)],
                                bias_ref[off, hp])
            o_ref[q0:q0 + GRID_W, LANES * hp:LANES * (hp + 1)] = o.astype(BF16)
        if j % 2 == 1:
            m0 = (j - 1) * GRID_W
            for hp in range(WIDTH_M // LANES):
                o, _ = _attend_pair(qm_ref[m0:m0 + 2 * GRID_W, LANES * hp:LANES * (hp + 1)],
                                    km_ref[:, LANES * hp:LANES * (hp + 1)],
                                    vm_ref[:, 2 * LANES * hp:2 * LANES * (hp + 1)], None)
                om_ref[m0:m0 + 2 * GRID_W, LANES * hp:LANES * (hp + 1)] = o.astype(BF16)


def _natten(q, k, v, bias, qm, km, vm):
    b, t, w = q.shape
    n_rows = t // GRID_W
    n_groups = n_rows // NA_GROUP
    blk = lambda f: pl.BlockSpec((None, NA_TOK, w), lambda bi, g: (bi, f(g), 0))
    cur = blk(lambda g: g)
    prev = blk(lambda g: jnp.maximum(g - 1, 0))
    nxt = blk(lambda g: jnp.minimum(g + 1, n_groups - 1))
    cur_m = pl.BlockSpec((None, NA_TOK, WIDTH_M), lambda bi, g: (bi, g, 0))
    per_b = lambda width: pl.BlockSpec((None, N_MEM, width), lambda bi, g: (bi, 0, 0))
    return pl.pallas_call(
        functools.partial(_natten_kernel, n_rows=n_rows),
        grid=(b, n_groups),
        in_specs=[cur, prev, cur, nxt, prev, cur, nxt, _const_spec(bias.shape),
                  cur_m, per_b(WIDTH_M), per_b(2 * WIDTH_M)],
        out_specs=[cur, cur_m],
        out_shape=[jax.ShapeDtypeStruct((b, t, w), BF16), jax.ShapeDtypeStruct((b, t, WIDTH_M), BF16)],
        scratch_shapes=[pltpu.VMEM((3 * NA_TOK, w), BF16), pltpu.VMEM((3 * NA_TOK, 2 * w), BF16)],
        compiler_params=_cparams(2, in_order=True),
        name="natten",
    )(q, k, k, k, v, v, v, bias, qm, km, vm)


def _natten_bias(rpb):
    c = np.arange(GRID_W)[:, None]
    kc = np.arange(GRID_W)[None, :]
    c0 = np.clip(c - NA_COLS // 2, 0, GRID_W - NA_COLS)
    valid = (kc >= c0) & (kc < c0 + NA_COLS)
    dcol = kc - c + NA_COLS - 1
    col_sel = (valid[..., None] & (dcol[..., None] == np.arange(2 * NA_COLS - 1))).astype(np.float32)
    cols = jnp.einsum("hrd,ckd->hcrk", rpb.astype(F32), col_sel, precision=lax.Precision.HIGHEST)
    cols = jnp.where(valid[None, :, None, :], cols * LOG2E, NEG)
    cols = cols.reshape(N_HEADS_B // 2, 2 * GRID_W, (2 * NA_ROWS - 1) * GRID_W)
    tab = jnp.stack([cols[:, :, (NA_ROWS - 1 - off) * GRID_W:(2 * NA_ROWS - 1 - off) * GRID_W]
                     for off in range(NA_ROWS)])
    return tab


FIN_TM = 512
FF_CHUNK = 1024


def _final_kernel(x_ref, o1_ref, o4_ref, o16_ref, l1_ref, l4_ref, l16_ref, ob_ref, om_ref,
                  ga_ref, gb_ref, gm_ref, wo_ref, gf_ref, w1_ref, w2_ref, y_ref,
                  *stages):
    tm = x_ref.shape[0]
    n_tiles = WIDTH_A // LANES
    rows4, rows16 = tm // DIL_STEP, tm // (DIL_STEP * DIL_STEP)

    def token_order(ref, stage, stage_b=None):
        src = lambda col: ref[:, col:col + LANES].astype(F32)
        if stage_b is not None:
            for r4 in range(DIL_STEP):
                for q in range(DIL_STEP):
                    for c in range(n_tiles):
                        stage_b[c, pl.ds(r4 * rows4 + q, rows16, stride=DIL_STEP), :] = (
                            src((r4 + DIL_STEP * q) * WIDTH_A + c * LANES))
        for r4 in range(DIL_STEP):
            for c in range(n_tiles):
                if stage_b is not None:
                    cls = stage_b[c, r4 * rows4:(r4 + 1) * rows4, :]
                else:
                    cls = src(r4 * WIDTH_A + c * LANES)
                stage[c, pl.ds(r4, rows4, stride=DIL_STEP), :] = cls
        return jnp.concatenate([stage[c] for c in range(n_tiles)], axis=1)

    o1, l1 = o1_ref[...].astype(F32), l1_ref[...]
    o4 = token_order(o4_ref, stages[0])
    l4 = token_order(l4_ref, stages[1])
    o16 = token_order(o16_ref, stages[2], stages[4])
    l16 = token_order(l16_ref, stages[3], stages[5])
    mx = jnp.maximum(jnp.maximum(l1, l4), l16)
    e1, e4, e16 = jnp.exp(l1 - mx), jnp.exp(l4 - mx), jnp.exp(l16 - mx)
    oa = (e1 * o1 + e4 * o4 + e16 * o16) / (e1 + e4 + e16)
    mixed = jnp.concatenate([_rms(oa, ga_ref[...]), _rms(ob_ref[...].astype(F32), gb_ref[...]),
                             _rms(om_ref[...].astype(F32), gm_ref[...])], axis=1).astype(BF16)
    x1 = x_ref[...] + jnp.dot(mixed, wo_ref[...], preferred_element_type=F32)
    hf = _rms(x1, gf_ref[...]).astype(BF16)
    acc = x1
    for c in range(0, D_FF, FF_CHUNK):
        h = jnp.dot(hf, w1_ref[:, c:c + FF_CHUNK], preferred_element_type=F32)
        h = jnp.square(jnp.maximum(h, 0.0)).astype(BF16)
        acc = acc + jnp.dot(h, w2_ref[c:c + FF_CHUNK, :], preferred_element_type=F32)
    y_ref[...] = acc


def _final(x, o1, o4, o16, l1, l4, l16, ob, om, ga, gb, gm, wo, gf, w1, w2):
    b, t, _ = x.shape
    tm = FIN_TM
    tok = lambda width: pl.BlockSpec((None, tm, width), lambda bi, i: (bi, i, 0))
    cls = [pl.BlockSpec((None, tm // d, d * WIDTH_A), lambda bi, i: (bi, i, 0)) for d in DILATIONS]
    return pl.pallas_call(
        _final_kernel,
        grid=(b, t // tm),
        in_specs=[tok(D_MODEL), *cls, *cls, tok(WIDTH_B), tok(WIDTH_M),
                  _const_spec((1, WIDTH_A)), _const_spec((1, WIDTH_B)), _const_spec((1, WIDTH_M)),
                  _const_spec((D_MODEL, D_MODEL)), _const_spec((1, D_MODEL)),
                  _const_spec((D_MODEL, D_FF)), _const_spec((D_FF, D_MODEL))],
        out_specs=tok(D_MODEL),
        out_shape=jax.ShapeDtypeStruct((b, t, D_MODEL), F32),
        scratch_shapes=[pltpu.VMEM((WIDTH_A // LANES, tm, LANES), F32)] * 6,
        compiler_params=_cparams(2),
        name="final",
    )(x, o1, o4, o16, l1, l4, l16, ob, om, ga, gb, gm, wo, gf, w1, w2)


def _rope_tables(t):
    half = ROPE_DIMS // 2
    inv = ROPE_THETA ** (-(np.arange(half, dtype=np.float64) * 2.0 / ROPE_DIMS))
    ang = np.arange(t, dtype=np.float64)[:, None] * inv[None, :]
    cs = jnp.asarray(np.concatenate([np.cos(ang), np.sin(ang)], axis=1), dtype=F32)
    cos, sin = cs[:, :half], cs[:, half:]
    rest = HEAD_DIM - ROPE_DIMS
    cos_h = jnp.concatenate([cos, cos, jnp.ones((t, rest), F32)], axis=1)
    sin_h = jnp.concatenate([-sin, sin, jnp.zeros((t, rest), F32)], axis=1)
    return jnp.tile(cos_h, (1, LANES // HEAD_DIM)), jnp.tile(sin_h, (1, LANES // HEAD_DIM))


def _group_columns(w):
    a, b = WIDTH_A, WIDTH_B
    qa, ka, va = w[:, :a], w[:, a:2 * a], w[:, 2 * a:3 * a]
    qb, kb, vb = w[:, 3 * a:3 * a + b], w[:, 3 * a + b:3 * a + 2 * b], w[:, 3 * a + 2 * b:3 * a + 3 * b]
    return jnp.concatenate([qa, ka, qb, kb, va, vb, w[:, 3 * a + 3 * b:]], axis=1)


def _layer(x, mem, p, tables):
    cos_t, sin_t = tables[x.shape[1]]
    qa, ka, va, (qb, kb, vb, qm) = _proj(x, p["g_attn"], p["w_in"], p["bd"], p["gqka"], p["gqkb"],
                                         p["gqm"], cos_t, sin_t)
    km, vm = _memkv(mem, p["g_mem"], p["w_kv"], p["bd"], p["gkm"])
    pats = [_dilated(qa[n], ka[n], va[n], p["band"], d) for n, d in enumerate(DILATIONS)]
    ob, om = _natten(qb, kb, vb, p["na_bias"], qm, km, vm)
    return _final(x, pats[0][0], pats[1][0], pats[2][0], pats[0][1], pats[1][1], pats[2][1],
                  ob, om, p["ga"], p["gb"], p["gm"], p["w_out"], p["g_ffn"], p["w1"], p["w2"])


def kernel(x_prompt, x_sample, mem_prompt, mem_sample, norm_attn, w_in, q_norm_a, k_norm_a, q_norm_b, k_norm_b, rpb_b, norm_mem, w_mem_kv, q_norm_m, k_norm_m, out_norm_a, out_norm_b, out_norm_m, w_out, norm_ffn, w_ff1, w_ff2):
    depth = w_in.shape[0]
    scale = HEAD_DIM ** -0.5 * LOG2E
    row = lambda v: v.astype(F32)[None, :]
    heads = lambda v, n: jnp.tile(v.astype(F32), n)[None, :]
    idx = np.arange(2 * LANES)
    bd = jnp.asarray(idx[:, None] // HEAD_DIM == idx[None, :] // HEAD_DIM, dtype=BF16)
    band = _band_bias()
    tables = {t: _rope_tables(t) for t in {x_prompt.shape[1], x_sample.shape[1]}}
    y_prompt, y_sample = x_prompt, x_sample
    for i in range(depth):
        p = dict(
            g_attn=row(norm_attn[i]), w_in=_group_columns(w_in[i]).astype(BF16), bd=bd, band=band,
            gqka=jnp.concatenate([heads(q_norm_a[i], N_HEADS_A) * scale, heads(k_norm_a[i], N_HEADS_A)], axis=1),
            gqkb=jnp.concatenate([heads(q_norm_b[i], N_HEADS_B) * scale, heads(k_norm_b[i], N_HEADS_B)], axis=1),
            gqm=heads(q_norm_m[i], N_HEADS_M) * scale, gkm=heads(k_norm_m[i], N_HEADS_M),
            na_bias=_natten_bias(rpb_b[i]),
            g_mem=row(norm_mem[i]), w_kv=w_mem_kv[i].astype(BF16),
            ga=row(out_norm_a[i]), gb=row(out_norm_b[i]), gm=row(out_norm_m[i]),
            w_out=w_out[i].astype(BF16), g_ffn=row(norm_ffn[i]),
            w1=w_ff1[i].astype(BF16), w2=w_ff2[i].astype(BF16))
        y_prompt = _layer(y_prompt, mem_prompt, p, tables)
        y_sample = _layer(y_sample, mem_sample, p, tables)
    return (y_prompt, y_sample)
```

```python
import functools

import numpy as np
import jax
import jax.numpy as jnp
from jax import lax
from jax.experimental import pallas as pl
from jax.experimental.pallas import tpu as pltpu

D_MODEL = 1024
HEAD_DIM = 64
N_HEADS_A = 6
N_HEADS_B = 6
N_HEADS_M = 4
WIDTH_A = N_HEADS_A * HEAD_DIM
WIDTH_B = N_HEADS_B * HEAD_DIM
WIDTH_M = N_HEADS_M * HEAD_DIM
IN_WIDTH = 3 * WIDTH_A + 3 * WIDTH_B + WIDTH_M
DILATIONS = (1, 4, 16)
DIL_STEP = 4
HALF = 64
GRID_W = 64
NA_ROWS = 8
NA_COLS = 16
N_MEM = 256
D_FF = 4 * D_MODEL
ROPE_THETA = 500000.0
ROPE_DIMS = HEAD_DIM // 4
EPS = 1e-6
NEG = -1e30
LOG2E = 1.4426950408889634
LN2 = 0.6931471805599453

LANES = 128
PAIR = 2 * HEAD_DIM
VMEM_LIMIT = 56 * 1024 * 1024

BF16 = jnp.bfloat16
F32 = jnp.float32


def _cparams(n_axes, in_order=False):
    return pltpu.CompilerParams(dimension_semantics=("arbitrary" if in_order else "parallel",) * n_axes,
                                vmem_limit_bytes=VMEM_LIMIT)


def _const_spec(shape):
    nd = len(shape)
    return pl.BlockSpec(shape, lambda *_: (0,) * nd, pipeline_mode=pl.Buffered(1))


def _rms(x, gain):
    return x * lax.rsqrt(jnp.mean(x * x, axis=-1, keepdims=True) + EPS) * gain


def _head_rms(z, bd, gain):
    ss = jnp.dot((z * z).astype(BF16), bd, preferred_element_type=F32)
    return z * lax.rsqrt(ss * (1.0 / HEAD_DIM) + EPS) * gain


def _attend_pair(qp, kp, vaug, bias):
    p, mx = _pair_probs(_pair_scores(qp, kp, bias))
    return _pair_output(p, mx, vaug)


def _pair_scores(qp, kp, bias):
    lane = lax.broadcasted_iota(jnp.int32, qp.shape, 1)
    zero = jnp.zeros_like(qp)
    lhs = jnp.concatenate([jnp.where(lane < HEAD_DIM, qp, zero),
                           jnp.where(lane >= HEAD_DIM, qp, zero)], axis=0)
    s = lax.dot_general(lhs, kp, (((1,), (1,)), ((), ())), preferred_element_type=F32)
    return s if bias is None else s + bias


def _pair_probs(s):
    mx = jnp.max(s, axis=-1, keepdims=True)
    return jnp.exp2(s - mx).astype(BF16), mx


def _pair_output(p, mx, vaug):
    m_rows = p.shape[0] // 2
    ov = jnp.dot(p, vaug, preferred_element_type=F32)
    first = lax.broadcasted_iota(jnp.int32, (m_rows, LANES), 1) < HEAD_DIM
    o = jnp.where(first, ov[:m_rows, :LANES], ov[m_rows:, :LANES])
    l = jnp.where(first, ov[:m_rows, LANES:], ov[m_rows:, LANES:])
    mm = jnp.where(first, mx[:m_rows], mx[m_rows:])
    return o / l, mm * LN2 + jnp.log(l)


def _fill_vaug(vaug_ref, row0, v, ones=True):
    rows = v.shape[0]
    for hp in range(v.shape[1] // LANES):
        vaug_ref[row0:row0 + rows, 2 * LANES * hp:2 * LANES * hp + LANES] = v[:, LANES * hp:LANES * (hp + 1)]
        if ones:
            vaug_ref[row0:row0 + rows, 2 * LANES * hp + LANES:2 * LANES * (hp + 1)] = jnp.ones((rows, LANES), BF16)


def _fill_ones(vaug_ref):
    for hp in range(vaug_ref.shape[1] // (2 * LANES)):
        vaug_ref[:, 2 * LANES * hp + LANES:2 * LANES * (hp + 1)] = jnp.ones((vaug_ref.shape[0], LANES), BF16)


def _first_step(n_axes):
    first = pl.program_id(0) == 0
    for axis in range(1, n_axes):
        first = first & (pl.program_id(axis) == 0)
    return first


PROJ_TM = 1024


def _proj_kernel(x_ref, g_ref, w_ref, bd_ref, gqka_ref, gqkb_ref, gqm_ref, cos_ref, sin_ref, *refs):
    n_dil = len(DILATIONS)
    qa_refs, ka_refs, va_refs = refs[:n_dil], refs[n_dil:2 * n_dil], refs[2 * n_dil:3 * n_dil]
    qb_ref, kb_ref, vb_ref, qm_ref = refs[3 * n_dil:3 * n_dil + 4]
    stages = refs[3 * n_dil + 4:]
    tm = x_ref.shape[0]
    x = x_ref[...]
    xn = _rms(x, g_ref[...]).astype(BF16)
    bd = bd_ref[...]

    def project(c0, width):
        return jnp.dot(xn, w_ref[:, c0:c0 + width], preferred_element_type=F32)

    def normed(z, gain_ref):
        return jnp.concatenate(
            [_head_rms(z[:, off:off + 2 * LANES], bd, gain_ref[:, off:off + 2 * LANES])
             for off in range(0, z.shape[1], 2 * LANES)], axis=1)

    def rope(y):
        cos = cos_ref[...]
        sin = sin_ref[...]
        lane = lax.broadcasted_iota(jnp.int32, cos.shape, 1) % HEAD_DIM
        low = lane < ROPE_DIMS // 2
        parts = []
        for c in range(0, y.shape[1], LANES):
            yc = y[:, c:c + LANES]
            partner = jnp.where(low, pltpu.roll(yc, LANES - ROPE_DIMS // 2, axis=1),
                                pltpu.roll(yc, ROPE_DIMS // 2, axis=1))
            parts.append(yc * cos + partner * sin)
        return jnp.concatenate(parts, axis=1)

    def emit_classes(y, out_refs, stage_a, stage_b):
        n_tiles = WIDTH_A // LANES
        rows4, rows16 = tm // DIL_STEP, tm // (DIL_STEP * DIL_STEP)

        def put(ref, r, c, val):
            col = (r * n_tiles + c) * LANES
            ref[:, col:col + LANES] = val.astype(BF16)

        for c in range(n_tiles):
            put(out_refs[0], 0, c, y[:, c * LANES:(c + 1) * LANES])
            stage_a[c] = y[:, c * LANES:(c + 1) * LANES]
        for r4 in range(DIL_STEP):
            for c in range(n_tiles):
                cls = stage_a[c, pl.ds(r4, rows4, stride=DIL_STEP), :]
                put(out_refs[1], r4, c, cls)
                stage_b[c, r4 * rows4:(r4 + 1) * rows4, :] = cls
        for r4 in range(DIL_STEP):
            for q in range(DIL_STEP):
                for c in range(n_tiles):
                    put(out_refs[2], r4 + DIL_STEP * q, c,
                        stage_b[c, pl.ds(r4 * rows4 + q, rows16, stride=DIL_STEP), :])

    wa, wb = WIDTH_A, WIDTH_B
    p_a = project(0, 2 * wa)
    p_b = project(2 * wa, 2 * wb)
    qk_a = rope(normed(p_a, gqka_ref))
    emit_classes(qk_a[:, :wa], qa_refs, stages[0], stages[1])
    emit_classes(qk_a[:, wa:], ka_refs, stages[2], stages[3])
    p_v = project(2 * wa + 2 * wb, wa + wb)
    qk_b = normed(p_b, gqkb_ref).astype(BF16)
    qb_ref[...] = qk_b[:, :wb]
    kb_ref[...] = qk_b[:, wb:]
    p_m = project(3 * wa + 3 * wb, WIDTH_M)
    emit_classes(p_v[:, :wa], va_refs, stages[4], stages[5])
    vb_ref[...] = p_v[:, wa:].astype(BF16)
    qm_ref[...] = normed(p_m, gqm_ref).astype(BF16)


def _proj(x, g, w_in, bd, gqka, gqkb, gqm, cos_t, sin_t):
    b, t, _ = x.shape
    tm = PROJ_TM
    tok = lambda width: pl.BlockSpec((None, tm, width), lambda bi, i: (bi, i, 0))
    cls = lambda dil, w: pl.BlockSpec((None, tm // dil, dil * w), lambda bi, i: (bi, i, 0))
    cls_shape = lambda dil, w: jax.ShapeDtypeStruct((b, t // dil, dil * w), BF16)
    cls_widths = (WIDTH_A, WIDTH_A, WIDTH_A)
    rest_widths = (WIDTH_B, WIDTH_B, WIDTH_B, WIDTH_M)
    n = len(DILATIONS)
    outs = pl.pallas_call(
        _proj_kernel,
        grid=(b, t // tm),
        in_specs=[tok(D_MODEL), _const_spec((1, D_MODEL)), _const_spec((D_MODEL, IN_WIDTH)),
                  _const_spec((2 * LANES, 2 * LANES)),
                  _const_spec((1, 2 * WIDTH_A)), _const_spec((1, 2 * WIDTH_B)), _const_spec((1, WIDTH_M)),
                  pl.BlockSpec((tm, LANES), lambda bi, i: (i, 0)),
                  pl.BlockSpec((tm, LANES), lambda bi, i: (i, 0))],
        out_specs=[cls(d, w) for w in cls_widths for d in DILATIONS] + [tok(w) for w in rest_widths],
        out_shape=[cls_shape(d, w) for w in cls_widths for d in DILATIONS]
                  + [jax.ShapeDtypeStruct((b, t, w), BF16) for w in rest_widths],
        scratch_shapes=[pltpu.VMEM((WIDTH_A // LANES, tm, LANES), F32)] * 6,
        compiler_params=_cparams(2),
        name="proj",
    )(x, g, w_in, bd, gqka, gqkb, gqm, cos_t, sin_t)
    return outs[:n], outs[n:2 * n], outs[2 * n:3 * n], outs[3 * n:]


def _memkv_kernel(mem_ref, g_ref, w_ref, bd_ref, gk_ref, km_ref, vaug_ref):
    mn = _rms(mem_ref[...], g_ref[...]).astype(BF16)
    kv = jnp.dot(mn, w_ref[...], preferred_element_type=F32)
    km_ref[...] = _head_rms(kv[:, :WIDTH_M], bd_ref[...], gk_ref[...]).astype(BF16)
    _fill_vaug(vaug_ref, 0, kv[:, WIDTH_M:].astype(BF16))


def _memkv(mem, g, w_kv, bd, gk):
    b = mem.shape[0]
    return pl.pallas_call(
        _memkv_kernel,
        grid=(b,),
        in_specs=[pl.BlockSpec((None, N_MEM, D_MODEL), lambda bi: (bi, 0, 0)),
                  _const_spec((1, D_MODEL)), _const_spec((D_MODEL, 2 * WIDTH_M)),
                  _const_spec((2 * LANES, 2 * LANES)), _const_spec((1, WIDTH_M))],
        out_specs=[pl.BlockSpec((None, N_MEM, WIDTH_M), lambda bi: (bi, 0, 0)),
                   pl.BlockSpec((None, N_MEM, 2 * WIDTH_M), lambda bi: (bi, 0, 0))],
        out_shape=[jax.ShapeDtypeStruct((b, N_MEM, WIDTH_M), BF16),
                   jax.ShapeDtypeStruct((b, N_MEM, 2 * WIDTH_M), BF16)],
        compiler_params=_cparams(1),
        name="memkv",
    )(mem, g, w_kv, bd, gk)


DIL_QB = 2048
DIL_SUB = 2 * HALF


def _dilated_kernel(q_ref, kp_ref, kc_ref, kn_ref, vp_ref, vc_ref, vn_ref, bias_ref,
                    o_ref, lse_ref, kbuf, vaug, *, n_sub_total):
    qb = q_ref.shape[0]
    kbuf[0:HALF] = kp_ref[...]
    kbuf[HALF:HALF + qb] = kc_ref[...]
    kbuf[HALF + qb:2 * HALF + qb] = kn_ref[...]
    _fill_vaug(vaug, 0, vp_ref[...])
    _fill_vaug(vaug, HALF, vc_ref[...])
    _fill_vaug(vaug, HALF + qb, vn_ref[...])
    n_sub = qb // DIL_SUB
    first_sub = pl.program_id(2) * n_sub

    for j in range(n_sub):
        gs = first_sub + j
        variant = jnp.where(gs == 0, 0, jnp.where(gs == n_sub_total - 1, 2, 1))
        r0 = j * DIL_SUB
        for hp in range(q_ref.shape[1] // LANES):
            o, lse = _attend_pair(q_ref[r0:r0 + DIL_SUB, LANES * hp:LANES * (hp + 1)],
                                  kbuf[r0:r0 + 2 * DIL_SUB, LANES * hp:LANES * (hp + 1)],
                                  vaug[r0:r0 + 2 * DIL_SUB, 2 * LANES * hp:2 * LANES * (hp + 1)],
                                  bias_ref[variant])
            o_ref[r0:r0 + DIL_SUB, LANES * hp:LANES * (hp + 1)] = o.astype(BF16)
            lse_ref[r0:r0 + DIL_SUB, LANES * hp:LANES * (hp + 1)] = lse


def _dilated(q, k, v, bias, dil):
    b, ln = q.shape[0], q.shape[1]
    qb = min(DIL_QB, ln)
    classes = min(dil, DIL_QB // qb)
    w = classes * (q.shape[2] // dil)
    hb = qb // HALF
    n_halo = ln // HALF
    main = pl.BlockSpec((None, qb, w), lambda bi, r, i: (bi, i, r))
    prev = pl.BlockSpec((None, HALF, w), lambda bi, r, i: (bi, jnp.maximum(i * hb - 1, 0), r))
    nxt = pl.BlockSpec((None, HALF, w), lambda bi, r, i: (bi, jnp.minimum((i + 1) * hb, n_halo - 1), r))
    o, lse = pl.pallas_call(
        functools.partial(_dilated_kernel, n_sub_total=ln // DIL_SUB),
        grid=(b, dil // classes, ln // qb),
        in_specs=[main, prev, main, nxt, prev, main, nxt, _const_spec(bias.shape)],
        out_specs=[main, main],
        out_shape=[jax.ShapeDtypeStruct(q.shape, BF16), jax.ShapeDtypeStruct(q.shape, F32)],
        scratch_shapes=[pltpu.VMEM((qb + 2 * HALF, w), BF16),
                        pltpu.VMEM((qb + 2 * HALF, 2 * w), BF16)],
        compiler_params=_cparams(3),
        name=f"dilated{dil}",
    )(q, k, k, k, v, v, v, bias)
    return o, lse


def _band_bias():
    row = np.arange(DIL_SUB)[:, None]
    col = np.arange(2 * DIL_SUB)[None, :]
    band = (col - row >= 0) & (col - row <= 2 * HALF)
    variants = [band & (col >= HALF), band, band & (col < 2 * DIL_SUB - HALF)]
    tab = np.stack([np.where(np.concatenate([m, m], axis=0), 0.0, NEG) for m in variants])
    return jnp.asarray(tab, dtype=F32)


NA_GROUP = 16
NA_TOK = NA_GROUP * GRID_W
NA_KEYS = NA_ROWS * GRID_W


def _natten_kernel(q_ref, kp_ref, kc_ref, kn_ref, vp_ref, vc_ref, vn_ref, bias_ref, qm_ref, km_ref, vm_ref,
                   o_ref, om_ref, kbuf, vaug, *, n_rows):
    kbuf[0:NA_TOK] = kp_ref[...]
    kbuf[NA_TOK:2 * NA_TOK] = kc_ref[...]
    kbuf[2 * NA_TOK:3 * NA_TOK] = kn_ref[...]
    pl.when(_first_step(2))(lambda: _fill_ones(vaug))
    _fill_vaug(vaug, 0, vp_ref[...], ones=False)
    _fill_vaug(vaug, NA_TOK, vc_ref[...], ones=False)
    _fill_vaug(vaug, 2 * NA_TOK, vn_ref[...], ones=False)
    g = pl.program_id(1)

    for j in range(NA_GROUP):
        r = g * NA_GROUP + j
        r0 = jnp.clip(r - NA_ROWS // 2, 0, n_rows - NA_ROWS)
        off = r - r0
        start = pl.multiple_of((r0 - (g - 1) * NA_GROUP) * GRID_W, GRID_W)
        q0 = j * GRID_W
        for hp in range(WIDTH_B // LANES):
            o, _ = _attend_pair(q_ref[q0:q0 + GRID_W, LANES * hp:LANES * (hp + 1)],
                                kbuf[pl.ds(start, NA_KEYS), LANES * hp:LANES * (hp + 1)],
                                vaug[pl.ds(start, NA_KEYS), 2 * LANES * hp:2 * LANES * (hp + 1)],
                                bias_ref[off, hp])
            o_ref[q0:q0 + GRID_W, LANES * hp:LANES * (hp + 1)] = o.astype(BF16)
        if j % 2 == 1:
            m0 = (j - 1) * GRID_W
            for hp in range(WIDTH_M // LANES):
                o, _ = _attend_pair(qm_ref[m0:m0 + 2 * GRID_W, LANES * hp:LANES * (hp + 1)],
                                    km_ref[:, LANES * hp:LANES * (hp + 1)],
                                    vm_ref[:, 2 * LANES * hp:2 * LANES * (hp + 1)], None)
                om_ref[m0:m0 + 2 * GRID_W, LANES * hp:LANES * (hp + 1)] = o.astype(BF16)


def _natten(q, k, v, bias, qm, km, vm):
    b, t, w = q.shape
    n_rows = t // GRID_W
    n_groups = n_rows // NA_GROUP
    blk = lambda f: pl.BlockSpec((None, NA_TOK, w), lambda bi, g: (bi, f(g), 0))
    cur = blk(lambda g: g)
    prev = blk(lambda g: jnp.maximum(g - 1, 0))
    nxt = blk(lambda g: jnp.minimum(g + 1, n_groups - 1))
    cur_m = pl.BlockSpec((None, NA_TOK, WIDTH_M), lambda bi, g: (bi, g, 0))
    per_b = lambda width: pl.BlockSpec((None, N_MEM, width), lambda bi, g: (bi, 0, 0))
    return pl.pallas_call(
        functools.partial(_natten_kernel, n_rows=n_rows),
        grid=(b, n_groups),
        in_specs=[cur, prev, cur, nxt, prev, cur, nxt, _const_spec(bias.shape),
                  cur_m, per_b(WIDTH_M), per_b(2 * WIDTH_M)],
        out_specs=[cur, cur_m],
        out_shape=[jax.ShapeDtypeStruct((b, t, w), BF16), jax.ShapeDtypeStruct((b, t, WIDTH_M), BF16)],
        scratch_shapes=[pltpu.VMEM((3 * NA_TOK, w), BF16), pltpu.VMEM((3 * NA_TOK, 2 * w), BF16)],
        compiler_params=_cparams(2, in_order=True),
        name="natten",
    )(q, k, k, k, v, v, v, bias, qm, km, vm)


def _natten_bias(rpb):
    c = np.arange(GRID_W)[:, None]
    kc = np.arange(GRID_W)[None, :]
    c0 = np.clip(c - NA_COLS // 2, 0, GRID_W - NA_COLS)
    valid = (kc >= c0) & (kc < c0 + NA_COLS)
    dcol = kc - c + NA_COLS - 1
    col_sel = (valid[..., None] & (dcol[..., None] == np.arange(2 * NA_COLS - 1))).astype(np.float32)
    cols = jnp.einsum("hrd,ckd->hcrk", rpb.astype(F32), col_sel, precision=lax.Precision.HIGHEST)
    cols = jnp.where(valid[None, :, None, :], cols * LOG2E, NEG)
    cols = cols.reshape(N_HEADS_B // 2, 2 * GRID_W, (2 * NA_ROWS - 1) * GRID_W)
    tab = jnp.stack([cols[:, :, (NA_ROWS - 1 - off) * GRID_W:(2 * NA_ROWS - 1 - off) * GRID_W]
                     for off in range(NA_ROWS)])
    return tab


FIN_TM = 512
FF_CHUNK = 1024


def _final_kernel(x_ref, o1_ref, o4_ref, o16_ref, l1_ref, l4_ref, l16_ref, ob_ref, om_ref,
                  ga_ref, gb_ref, gm_ref, wo_ref, gf_ref, w1_ref, w2_ref, y_ref,
                  *stages):
    tm = x_ref.shape[0]
    n_tiles = WIDTH_A // LANES
    rows4, rows16 = tm // DIL_STEP, tm // (DIL_STEP * DIL_STEP)

    def token_order(ref, stage, stage_b=None):
        src = lambda col: ref[:, col:col + LANES].astype(F32)
        if stage_b is not None:
            for r4 in range(DIL_STEP):
                for q in range(DIL_STEP):
                    for c in range(n_tiles):
                        stage_b[c, pl.ds(r4 * rows4 + q, rows16, stride=DIL_STEP), :] = (
                            src((r4 + DIL_STEP * q) * WIDTH_A + c * LANES))
        for r4 in range(DIL_STEP):
            for c in range(n_tiles):
                if stage_b is not None:
                    cls = stage_b[c, r4 * rows4:(r4 + 1) * rows4, :]
                else:
                    cls = src(r4 * WIDTH_A + c * LANES)
                stage[c, pl.ds(r4, rows4, stride=DIL_STEP), :] = cls
        return jnp.concatenate([stage[c] for c in range(n_tiles)], axis=1)

    o1, l1 = o1_ref[...].astype(F32), l1_ref[...]
    o4 = token_order(o4_ref, stages[0])
    l4 = token_order(l4_ref, stages[1])
    o16 = token_order(o16_ref, stages[2], stages[4])
    l16 = token_order(l16_ref, stages[3], stages[5])
    mx = jnp.maximum(jnp.maximum(l1, l4), l16)
    e1, e4, e16 = jnp.exp(l1 - mx), jnp.exp(l4 - mx), jnp.exp(l16 - mx)
    oa = (e1 * o1 + e4 * o4 + e16 * o16) / (e1 + e4 + e16)
    mixed = jnp.concatenate([_rms(oa, ga_ref[...]), _rms(ob_ref[...].astype(F32), gb_ref[...]),
                             _rms(om_ref[...].astype(F32), gm_ref[...])], axis=1).astype(BF16)
    x1 = x_ref[...] + jnp.dot(mixed, wo_ref[...], preferred_element_type=F32)
    hf = _rms(x1, gf_ref[...]).astype(BF16)
    acc = x1
    for c in range(0, D_FF, FF_CHUNK):
        h = jnp.dot(hf, w1_ref[:, c:c + FF_CHUNK], preferred_element_type=F32)
        h = jnp.square(jnp.maximum(h, 0.0)).astype(BF16)
        acc = acc + jnp.dot(h, w2_ref[c:c + FF_CHUNK, :], preferred_element_type=F32)
    y_ref[...] = acc


def _final(x, o1, o4, o16, l1, l4, l16, ob, om, ga, gb, gm, wo, gf, w1, w2):
    b, t, _ = x.shape
    tm = FIN_TM
    tok = lambda width: pl.BlockSpec((None, tm, width), lambda bi, i: (bi, i, 0))
    cls = [pl.BlockSpec((None, tm // d, d * WIDTH_A), lambda bi, i: (bi, i, 0)) for d in DILATIONS]
    return pl.pallas_call(
        _final_kernel,
        grid=(b, t // tm),
        in_specs=[tok(D_MODEL), *cls, *cls, tok(WIDTH_B), tok(WIDTH_M),
                  _const_spec((1, WIDTH_A)), _const_spec((1, WIDTH_B)), _const_spec((1, WIDTH_M)),
                  _const_spec((D_MODEL, D_MODEL)), _const_spec((1, D_MODEL)),
                  _const_spec((D_MODEL, D_FF)), _const_spec((D_FF, D_MODEL))],
        out_specs=tok(D_MODEL),
        out_shape=jax.ShapeDtypeStruct((b, t, D_MODEL), F32),
        scratch_shapes=[pltpu.VMEM((WIDTH_A // LANES, tm, LANES), F32)] * 6,
        compiler_params=_cparams(2),
        name="final",
    )(x, o1, o4, o16, l1, l4, l16, ob, om, ga, gb, gm, wo, gf, w1, w2)


def _rope_tables(t):
    half = ROPE_DIMS // 2
    inv = ROPE_THETA ** (-(np.arange(half, dtype=np.float64) * 2.0 / ROPE_DIMS))
    ang = np.arange(t, dtype=np.float64)[:, None] * inv[None, :]
    cs = jnp.asarray(np.concatenate([np.cos(ang), np.sin(ang)], axis=1), dtype=F32)
    cos, sin = cs[:, :half], cs[:, half:]
    rest = HEAD_DIM - ROPE_DIMS
    cos_h = jnp.concatenate([cos, cos, jnp.ones((t, rest), F32)], axis=1)
    sin_h = jnp.concatenate([-sin, sin, jnp.zeros((t, rest), F32)], axis=1)
    return jnp.tile(cos_h, (1, LANES // HEAD_DIM)), jnp.tile(sin_h, (1, LANES // HEAD_DIM))


def _group_columns(w):
    a, b = WIDTH_A, WIDTH_B
    qa, ka, va = w[:, :a], w[:, a:2 * a], w[:, 2 * a:3 * a]
    qb, kb, vb = w[:, 3 * a:3 * a + b], w[:, 3 * a + b:3 * a + 2 * b], w[:, 3 * a + 2 * b:3 * a + 3 * b]
    return jnp.concatenate([qa, ka, qb, kb, va, vb, w[:, 3 * a + 3 * b:]], axis=1)


def _layer(x, mem, p, tables):
    cos_t, sin_t = tables[x.shape[1]]
    qa, ka, va, (qb, kb, vb, qm) = _proj(x, p["g_attn"], p["w_in"], p["bd"], p["gqka"], p["gqkb"],
                                         p["gqm"], cos_t, sin_t)
    km, vm = _memkv(mem, p["g_mem"], p["w_kv"], p["bd"], p["gkm"])
    pats = [_dilated(qa[n], ka[n], va[n], p["band"], d) for n, d in enumerate(DILATIONS)]
    ob, om = _natten(qb, kb, vb, p["na_bias"], qm, km, vm)
    return _final(x, pats[0][0], pats[1][0], pats[2][0], pats[0][1], pats[1][1], pats[2][1],
                  ob, om, p["ga"], p["gb"], p["gm"], p["w_out"], p["g_ffn"], p["w1"], p["w2"])


def kernel(x_prompt, x_sample, mem_prompt, mem_sample, norm_attn, w_in, q_norm_a, k_norm_a, q_norm_b, k_norm_b, rpb_b, norm_mem, w_mem_kv, q_norm_m, k_norm_m, out_norm_a, out_norm_b, out_norm_m, w_out, norm_ffn, w_ff1, w_ff2):
    depth = w_in.shape[0]
    scale = HEAD_DIM ** -0.5 * LOG2E
    row = lambda v: v.astype(F32)[None, :]
    heads = lambda v, n: jnp.tile(v.astype(F32), n)[None, :]
    idx = np.arange(2 * LANES)
    bd = jnp.asarray(idx[:, None] // HEAD_DIM == idx[None, :] // HEAD_DIM, dtype=BF16)
    band = _band_bias()
    tables = {t: _rope_tables(t) for t in {x_prompt.shape[1], x_sample.shape[1]}}
    y_prompt, y_sample = x_prompt, x_sample
    for i in range(depth):
        p = dict(
            g_attn=row(norm_attn[i]), w_in=_group_columns(w_in[i]).astype(BF16), bd=bd, band=band,
            gqka=jnp.concatenate([heads(q_norm_a[i], N_HEADS_A) * scale, heads(k_norm_a[i], N_HEADS_A)], axis=1),
            gqkb=jnp.concatenate([heads(q_norm_b[i], N_HEADS_B) * scale, heads(k_norm_b[i], N_HEADS_B)], axis=1),
            gqm=heads(q_norm_m[i], N_HEADS_M) * scale, gkm=heads(k_norm_m[i], N_HEADS_M),
            na_bias=_natten_bias(rpb_b[i]),
            g_mem=row(norm_mem[i]), w_kv=w_mem_kv[i].astype(BF16),
            ga=row(out_norm_a[i]), gb=row(out_norm_b[i]), gm=row(out_norm_m[i]),
            w_out=w_out[i].astype(BF16), g_ffn=row(norm_ffn[i]),
            w1=w_ff1[i].astype(BF16), w2=w_ff2[i].astype(BF16))
        y_prompt = _layer(y_prompt, mem_prompt, p, tables)
        y_sample = _layer(y_sample, mem_sample, p, tables)
    return (y_prompt, y_sample)
```

```python
import functools

import numpy as np
import jax
import jax.numpy as jnp
from jax import lax
from jax.experimental import pallas as pl
from jax.experimental.pallas import tpu as pltpu

D_MODEL = 1024
HEAD_DIM = 64
N_HEADS_A = 6
N_HEADS_B = 6
N_HEADS_M = 4
WIDTH_A = N_HEADS_A * HEAD_DIM
WIDTH_B = N_HEADS_B * HEAD_DIM
WIDTH_M = N_HEADS_M * HEAD_DIM
IN_WIDTH = 3 * WIDTH_A + 3 * WIDTH_B + WIDTH_M
DILATIONS = (1, 4, 16)
DIL_STEP = 4
HALF = 64
GRID_W = 64
NA_ROWS = 8
NA_COLS = 16
N_MEM = 256
D_FF = 4 * D_MODEL
ROPE_THETA = 500000.0
ROPE_DIMS = HEAD_DIM // 4
EPS = 1e-6
NEG = -1e30
LOG2E = 1.4426950408889634
LN2 = 0.6931471805599453

LANES = 128
MXU_TILE = 256
VMEM_LIMIT = 56 * 1024 * 1024

BF16 = jnp.bfloat16
F32 = jnp.float32


def _cparams(n_axes, in_order=False):
    return pltpu.CompilerParams(dimension_semantics=("arbitrary" if in_order else "parallel",) * n_axes,
                                vmem_limit_bytes=VMEM_LIMIT)


def _const_spec(shape):
    nd = len(shape)
    return pl.BlockSpec(shape, lambda *_: (0,) * nd, pipeline_mode=pl.Buffered(1))


def _rms(x, gain):
    return x * lax.rsqrt(jnp.mean(x * x, axis=-1, keepdims=True) + EPS) * gain


def _head_rms(z, bd, gain):
    ss = jnp.dot((z * z).astype(BF16), bd, preferred_element_type=F32)
    return z * lax.rsqrt(ss * (1.0 / HEAD_DIM) + EPS) * gain


def _attend_pair(qp, kp, vaug, bias):
    p, mx = _pair_probs(_pair_scores(qp, kp, bias))
    return _pair_output(p, mx, vaug)


def _pair_scores(qp, kp, bias):
    lane = lax.broadcasted_iota(jnp.int32, qp.shape, 1)
    zero = jnp.zeros_like(qp)
    lhs = jnp.concatenate([jnp.where(lane < HEAD_DIM, qp, zero),
                           jnp.where(lane >= HEAD_DIM, qp, zero)], axis=0)
    s = lax.dot_general(lhs, kp, (((1,), (1,)), ((), ())), preferred_element_type=F32)
    return s if bias is None else s + bias


def _pair_probs(s):
    mx = jnp.max(s, axis=-1, keepdims=True)
    return jnp.exp2(s - mx).astype(BF16), mx


def _pair_output(p, mx, vaug):
    m_rows = p.shape[0] // 2
    ov = jnp.dot(p, vaug, preferred_element_type=F32)
    first = lax.broadcasted_iota(jnp.int32, (m_rows, LANES), 1) < HEAD_DIM
    o = jnp.where(first, ov[:m_rows, :LANES], ov[m_rows:, :LANES])
    l = jnp.where(first, ov[:m_rows, LANES:], ov[m_rows:, LANES:])
    mm = jnp.where(first, mx[:m_rows], mx[m_rows:])
    return o / l, mm * LN2 + jnp.log(l)


def _fill_vaug(vaug_ref, row0, v, ones=True):
    rows = v.shape[0]
    for hp in range(v.shape[1] // LANES):
        vaug_ref[row0:row0 + rows, 2 * LANES * hp:2 * LANES * hp + LANES] = v[:, LANES * hp:LANES * (hp + 1)]
        if ones:
            vaug_ref[row0:row0 + rows, 2 * LANES * hp + LANES:2 * LANES * (hp + 1)] = jnp.ones((rows, LANES), BF16)


def _fill_ones(vaug_ref):
    for hp in range(vaug_ref.shape[1] // (2 * LANES)):
        vaug_ref[:, 2 * LANES * hp + LANES:2 * LANES * (hp + 1)] = jnp.ones((vaug_ref.shape[0], LANES), BF16)


def _first_step(n_axes):
    first = pl.program_id(0) == 0
    for axis in range(1, n_axes):
        first = first & (pl.program_id(axis) == 0)
    return first


PROJ_TM = 1024


def _proj_kernel(x_ref, g_ref, w_ref, bd_ref, gqka_ref, gqkb_ref, gqm_ref, cos_ref, sin_ref, *refs):
    n_dil = len(DILATIONS)
    qa_refs, ka_refs, va_refs = refs[:n_dil], refs[n_dil:2 * n_dil], refs[2 * n_dil:3 * n_dil]
    qb_ref, kb_ref, vb_ref, qm_ref = refs[3 * n_dil:3 * n_dil + 4]
    stages = refs[3 * n_dil + 4:]
    tm = x_ref.shape[0]
    x = x_ref[...]
    xn = _rms(x, g_ref[...]).astype(BF16)
    bd = bd_ref[...]

    def project(c0, width):
        return jnp.dot(xn, w_ref[:, c0:c0 + width], preferred_element_type=F32)

    def normed(z, gain_ref):
        return jnp.concatenate(
            [_head_rms(z[:, off:off + MXU_TILE], bd, gain_ref[:, off:off + MXU_TILE])
             for off in range(0, z.shape[1], MXU_TILE)], axis=1)

    def rope(y):
        cos = cos_ref[...]
        sin = sin_ref[...]
        lane = lax.broadcasted_iota(jnp.int32, cos.shape, 1) % HEAD_DIM
        low = lane < ROPE_DIMS // 2
        parts = []
        for c in range(0, y.shape[1], LANES):
            yc = y[:, c:c + LANES]
            partner = jnp.where(low, pltpu.roll(yc, LANES - ROPE_DIMS // 2, axis=1),
                                pltpu.roll(yc, ROPE_DIMS // 2, axis=1))
            parts.append(yc * cos + partner * sin)
        return jnp.concatenate(parts, axis=1)

    def emit_classes(y, out_refs, stage_a, stage_b):
        n_tiles = WIDTH_A // LANES
        rows4, rows16 = tm // DIL_STEP, tm // (DIL_STEP * DIL_STEP)

        def put(ref, r, c, val):
            col = (r * n_tiles + c) * LANES
            ref[:, col:col + LANES] = val.astype(BF16)

        for c in range(n_tiles):
            put(out_refs[0], 0, c, y[:, c * LANES:(c + 1) * LANES])
            stage_a[c] = y[:, c * LANES:(c + 1) * LANES]
        for r4 in range(DIL_STEP):
            for c in range(n_tiles):
                cls = stage_a[c, pl.ds(r4, rows4, stride=DIL_STEP), :]
                put(out_refs[1], r4, c, cls)
                stage_b[c, r4 * rows4:(r4 + 1) * rows4, :] = cls
        for r4 in range(DIL_STEP):
            for q in range(DIL_STEP):
                for c in range(n_tiles):
                    put(out_refs[2], r4 + DIL_STEP * q, c,
                        stage_b[c, pl.ds(r4 * rows4 + q, rows16, stride=DIL_STEP), :])

    wa, wb = WIDTH_A, WIDTH_B
    p_a = project(0, 2 * wa)
    p_b = project(2 * wa, 2 * wb)
    qk_a = rope(normed(p_a, gqka_ref))
    emit_classes(qk_a[:, :wa], qa_refs, stages[0], stages[1])
    emit_classes(qk_a[:, wa:], ka_refs, stages[2], stages[3])
    p_v = project(2 * wa + 2 * wb, wa + wb)
    qk_b = normed(p_b, gqkb_ref).astype(BF16)
    qb_ref[...] = qk_b[:, :wb]
    kb_ref[...] = qk_b[:, wb:]
    p_m = project(3 * wa + 3 * wb, WIDTH_M)
    emit_classes(p_v[:, :wa], va_refs, stages[4], stages[5])
    vb_ref[...] = p_v[:, wa:].astype(BF16)
    qm_ref[...] = normed(p_m, gqm_ref).astype(BF16)


def _proj(x, g, w_in, bd, gqka, gqkb, gqm, cos_t, sin_t):
    b, t, _ = x.shape
    tm = PROJ_TM
    tok = lambda width: pl.BlockSpec((None, tm, width), lambda bi, i: (bi, i, 0))
    cls = lambda dil, w: pl.BlockSpec((None, tm // dil, dil * w), lambda bi, i: (bi, i, 0))
    cls_shape = lambda dil, w: jax.ShapeDtypeStruct((b, t // dil, dil * w), BF16)
    cls_widths = (WIDTH_A, WIDTH_A, WIDTH_A)
    rest_widths = (WIDTH_B, WIDTH_B, WIDTH_B, WIDTH_M)
    n = len(DILATIONS)
    outs = pl.pallas_call(
        _proj_kernel,
        grid=(b, t // tm),
        in_specs=[tok(D_MODEL), _const_spec((1, D_MODEL)), _const_spec((D_MODEL, IN_WIDTH)),
                  _const_spec((MXU_TILE, MXU_TILE)),
                  _const_spec((1, 2 * WIDTH_A)), _const_spec((1, 2 * WIDTH_B)), _const_spec((1, WIDTH_M)),
                  pl.BlockSpec((tm, LANES), lambda bi, i: (i, 0)),
                  pl.BlockSpec((tm, LANES), lambda bi, i: (i, 0))],
        out_specs=[cls(d, w) for w in cls_widths for d in DILATIONS] + [tok(w) for w in rest_widths],
        out_shape=[cls_shape(d, w) for w in cls_widths for d in DILATIONS]
                  + [jax.ShapeDtypeStruct((b, t, w), BF16) for w in rest_widths],
        scratch_shapes=[pltpu.VMEM((WIDTH_A // LANES, tm, LANES), F32)] * 6,
        compiler_params=_cparams(2),
        name="proj",
    )(x, g, w_in, bd, gqka, gqkb, gqm, cos_t, sin_t)
    return outs[:n], outs[n:2 * n], outs[2 * n:3 * n], outs[3 * n:]


def _memkv_kernel(mem_ref, g_ref, w_ref, bd_ref, gk_ref, km_ref, vaug_ref):
    mn = _rms(mem_ref[...], g_ref[...]).astype(BF16)
    kv = jnp.dot(mn, w_ref[...], preferred_element_type=F32)
    km_ref[...] = _head_rms(kv[:, :WIDTH_M], bd_ref[...], gk_ref[...]).astype(BF16)
    _fill_vaug(vaug_ref, 0, kv[:, WIDTH_M:].astype(BF16))


def _memkv(mem, g, w_kv, bd, gk):
    b = mem.shape[0]
    return pl.pallas_call(
        _memkv_kernel,
        grid=(b,),
        in_specs=[pl.BlockSpec((None, N_MEM, D_MODEL), lambda bi: (bi, 0, 0)),
                  _const_spec((1, D_MODEL)), _const_spec((D_MODEL, 2 * WIDTH_M)),
                  _const_spec((MXU_TILE, MXU_TILE)), _const_spec((1, WIDTH_M))],
        out_specs=[pl.BlockSpec((None, N_MEM, WIDTH_M), lambda bi: (bi, 0, 0)),
                   pl.BlockSpec((None, N_MEM, 2 * WIDTH_M), lambda bi: (bi, 0, 0))],
        out_shape=[jax.ShapeDtypeStruct((b, N_MEM, WIDTH_M), BF16),
                   jax.ShapeDtypeStruct((b, N_MEM, 2 * WIDTH_M), BF16)],
        compiler_params=_cparams(1),
        name="memkv",
    )(mem, g, w_kv, bd, gk)


DIL_QB = 2048
DIL_SUB = 2 * HALF


def _dilated_kernel(q_ref, kp_ref, kc_ref, kn_ref, vp_ref, vc_ref, vn_ref, bias_ref,
                    o_ref, lse_ref, kbuf, vaug, *, n_sub_total):
    qb = q_ref.shape[0]
    kbuf[0:HALF] = kp_ref[...]
    kbuf[HALF:HALF + qb] = kc_ref[...]
    kbuf[HALF + qb:2 * HALF + qb] = kn_ref[...]
    _fill_vaug(vaug, 0, vp_ref[...])
    _fill_vaug(vaug, HALF, vc_ref[...])
    _fill_vaug(vaug, HALF + qb, vn_ref[...])
    n_sub = qb // DIL_SUB
    first_sub = pl.program_id(2) * n_sub

    for j in range(n_sub):
        gs = first_sub + j
        variant = jnp.where(gs == 0, 0, jnp.where(gs == n_sub_total - 1, 2, 1))
        r0 = j * DIL_SUB
        for hp in range(q_ref.shape[1] // LANES):
            o, lse = _attend_pair(q_ref[r0:r0 + DIL_SUB, LANES * hp:LANES * (hp + 1)],
                                  kbuf[r0:r0 + 2 * DIL_SUB, LANES * hp:LANES * (hp + 1)],
                                  vaug[r0:r0 + 2 * DIL_SUB, 2 * LANES * hp:2 * LANES * (hp + 1)],
                                  bias_ref[variant])
            o_ref[r0:r0 + DIL_SUB, LANES * hp:LANES * (hp + 1)] = o.astype(BF16)
            lse_ref[r0:r0 + DIL_SUB, LANES * hp:LANES * (hp + 1)] = lse


def _dilated(q, k, v, bias, dil):
    b, ln = q.shape[0], q.shape[1]
    qb = min(DIL_QB, ln)
    classes = min(dil, DIL_QB // qb)
    w = classes * (q.shape[2] // dil)
    hb = qb // HALF
    n_halo = ln // HALF
    main = pl.BlockSpec((None, qb, w), lambda bi, r, i: (bi, i, r))
    prev = pl.BlockSpec((None, HALF, w), lambda bi, r, i: (bi, jnp.maximum(i * hb - 1, 0), r))
    nxt = pl.BlockSpec((None, HALF, w), lambda bi, r, i: (bi, jnp.minimum((i + 1) * hb, n_halo - 1), r))
    o, lse = pl.pallas_call(
        functools.partial(_dilated_kernel, n_sub_total=ln // DIL_SUB),
        grid=(b, dil // classes, ln // qb),
        in_specs=[main, prev, main, nxt, prev, main, nxt, _const_spec(bias.shape)],
        out_specs=[main, main],
        out_shape=[jax.ShapeDtypeStruct(q.shape, BF16), jax.ShapeDtypeStruct(q.shape, F32)],
        scratch_shapes=[pltpu.VMEM((qb + 2 * HALF, w), BF16),
                        pltpu.VMEM((qb + 2 * HALF, 2 * w), BF16)],
        compiler_params=_cparams(3),
        name=f"dilated{dil}",
    )(q, k, k, k, v, v, v, bias)
    return o, lse


def _band_bias():
    row = np.arange(DIL_SUB)[:, None]
    col = np.arange(2 * DIL_SUB)[None, :]
    band = (col - row >= 0) & (col - row <= 2 * HALF)
    variants = [band & (col >= HALF), band, band & (col < 2 * DIL_SUB - HALF)]
    tab = np.stack([np.where(np.concatenate([m, m], axis=0), 0.0, NEG) for m in variants])
    return jnp.asarray(tab, dtype=F32)


NA_GROUP = 16
NA_TOK = NA_GROUP * GRID_W
NA_KEYS = NA_ROWS * GRID_W


def _natten_kernel(q_ref, kp_ref, kc_ref, kn_ref, vp_ref, vc_ref, vn_ref, bias_ref, qm_ref, km_ref, vm_ref,
                   o_ref, om_ref, kbuf, vaug, *, n_rows):
    kbuf[0:NA_TOK] = kp_ref[...]
    kbuf[NA_TOK:2 * NA_TOK] = kc_ref[...]
    kbuf[2 * NA_TOK:3 * NA_TOK] = kn_ref[...]
    pl.when(_first_step(2))(lambda: _fill_ones(vaug))
    _fill_vaug(vaug, 0, vp_ref[...], ones=False)
    _fill_vaug(vaug, NA_TOK, vc_ref[...], ones=False)
    _fill_vaug(vaug, 2 * NA_TOK, vn_ref[...], ones=False)
    g = pl.program_id(1)

    for j in range(NA_GROUP):
        r = g * NA_GROUP + j
        r0 = jnp.clip(r - NA_ROWS // 2, 0, n_rows - NA_ROWS)
        off = r - r0
        start = pl.multiple_of((r0 - (g - 1) * NA_GROUP) * GRID_W, GRID_W)
        q0 = j * GRID_W
        for hp in range(WIDTH_B // LANES):
            o, _ = _attend_pair(q_ref[q0:q0 + GRID_W, LANES * hp:LANES * (hp + 1)],
                                kbuf[pl.ds(start, NA_KEYS), LANES * hp:LANES * (hp + 1)],
                                vaug[pl.ds(start, NA_KEYS), 2 * LANES * hp:2 * LANES * (hp + 1)],
                                bias_ref[off, hp])
            o_ref[q0:q0 + GRID_W, LANES * hp:LANES * (hp + 1)] = o.astype(BF16)
        if j % 2 == 1:
            m0 = (j - 1) * GRID_W
            for hp in range(WIDTH_M // LANES):
                o, _ = _attend_pair(qm_ref[m0:m0 + 2 * GRID_W, LANES * hp:LANES * (hp + 1)],
                                    km_ref[:, LANES * hp:LANES * (hp + 1)],
                                    vm_ref[:, 2 * LANES * hp:2 * LANES * (hp + 1)], None)
                om_ref[m0:m0 + 2 * GRID_W, LANES * hp:LANES * (hp + 1)] = o.astype(BF16)


def _natten(q, k, v, bias, qm, km, vm):
    b, t, w = q.shape
    n_rows = t // GRID_W
    n_groups = n_rows // NA_GROUP
    blk = lambda f: pl.BlockSpec((None, NA_TOK, w), lambda bi, g: (bi, f(g), 0))
    cur = blk(lambda g: g)
    prev = blk(lambda g: jnp.maximum(g - 1, 0))
    nxt = blk(lambda g: jnp.minimum(g + 1, n_groups - 1))
    cur_m = pl.BlockSpec((None, NA_TOK, WIDTH_M), lambda bi, g: (bi, g, 0))
    per_b = lambda width: pl.BlockSpec((None, N_MEM, width), lambda bi, g: (bi, 0, 0))
    return pl.pallas_call(
        functools.partial(_natten_kernel, n_rows=n_rows),
        grid=(b, n_groups),
        in_specs=[cur, prev, cur, nxt, prev, cur, nxt, _const_spec(bias.shape),
                  cur_m, per_b(WIDTH_M), per_b(2 * WIDTH_M)],
        out_specs=[cur, cur_m],
        out_shape=[jax.ShapeDtypeStruct((b, t, w), BF16), jax.ShapeDtypeStruct((b, t, WIDTH_M), BF16)],
        scratch_shapes=[pltpu.VMEM((3 * NA_TOK, w), BF16), pltpu.VMEM((3 * NA_TOK, 2 * w), BF16)],
        compiler_params=_cparams(2, in_order=True),
        name="natten",
    )(q, k, k, k, v, v, v, bias, qm, km, vm)


def _natten_bias(rpb):
    c = np.arange(GRID_W)[:, None]
    kc = np.arange(GRID_W)[None, :]
    c0 = np.clip(c - NA_COLS // 2, 0, GRID_W - NA_COLS)
    valid = (kc >= c0) & (kc < c0 + NA_COLS)
    dcol = kc - c + NA_COLS - 1
    col_sel = (valid[..., None] & (dcol[..., None] == np.arange(2 * NA_COLS - 1))).astype(np.float32)
    cols = jnp.einsum("hrd,ckd->hcrk", rpb.astype(F32), col_sel, precision=lax.Precision.HIGHEST)
    cols = jnp.where(valid[None, :, None, :], cols * LOG2E, NEG)
    cols = cols.reshape(N_HEADS_B // 2, 2 * GRID_W, (2 * NA_ROWS - 1) * GRID_W)
    tab = jnp.stack([cols[:, :, (NA_ROWS - 1 - off) * GRID_W:(2 * NA_ROWS - 1 - off) * GRID_W]
                     for off in range(NA_ROWS)])
    return tab


FIN_TM = 512
FF_CHUNK = 1024


def _final_kernel(x_ref, o1_ref, o4_ref, o16_ref, l1_ref, l4_ref, l16_ref, ob_ref, om_ref,
                  ga_ref, gb_ref, gm_ref, wo_ref, gf_ref, w1_ref, w2_ref, y_ref,
                  *stages):
    tm = x_ref.shape[0]
    n_tiles = WIDTH_A // LANES
    rows4, rows16 = tm // DIL_STEP, tm // (DIL_STEP * DIL_STEP)

    def token_order(ref, stage, stage_b=None):
        src = lambda col: ref[:, col:col + LANES].astype(F32)
        if stage_b is not None:
            for r4 in range(DIL_STEP):
                for q in range(DIL_STEP):
                    for c in range(n_tiles):
                        stage_b[c, pl.ds(r4 * rows4 + q, rows16, stride=DIL_STEP), :] = (
                            src((r4 + DIL_STEP * q) * WIDTH_A + c * LANES))
        for r4 in range(DIL_STEP):
            for c in range(n_tiles):
                if stage_b is not None:
                    cls = stage_b[c, r4 * rows4:(r4 + 1) * rows4, :]
                else:
                    cls = src(r4 * WIDTH_A + c * LANES)
                stage[c, pl.ds(r4, rows4, stride=DIL_STEP), :] = cls
        return jnp.concatenate([stage[c] for c in range(n_tiles)], axis=1)

    o1, l1 = o1_ref[...].astype(F32), l1_ref[...]
    o4 = token_order(o4_ref, stages[0])
    l4 = token_order(l4_ref, stages[1])
    o16 = token_order(o16_ref, stages[2], stages[4])
    l16 = token_order(l16_ref, stages[3], stages[5])
    mx = jnp.maximum(jnp.maximum(l1, l4), l16)
    e1, e4, e16 = jnp.exp(l1 - mx), jnp.exp(l4 - mx), jnp.exp(l16 - mx)
    oa = (e1 * o1 + e4 * o4 + e16 * o16) / (e1 + e4 + e16)
    mixed = jnp.concatenate([_rms(oa, ga_ref[...]), _rms(ob_ref[...].astype(F32), gb_ref[...]),
                             _rms(om_ref[...].astype(F32), gm_ref[...])], axis=1).astype(BF16)
    x1 = x_ref[...] + jnp.dot(mixed, wo_ref[...], preferred_element_type=F32)
    hf = _rms(x1, gf_ref[...]).astype(BF16)
    acc = x1
    for c in range(0, D_FF, FF_CHUNK):
        h = jnp.dot(hf, w1_ref[:, c:c + FF_CHUNK], preferred_element_type=F32)
        h = jnp.square(jnp.maximum(h, 0.0)).astype(BF16)
        acc = acc + jnp.dot(h, w2_ref[c:c + FF_CHUNK, :], preferred_element_type=F32)
    y_ref[...] = acc


def _final(x, o1, o4, o16, l1, l4, l16, ob, om, ga, gb, gm, wo, gf, w1, w2):
    b, t, _ = x.shape
    tm = FIN_TM
    tok = lambda width: pl.BlockSpec((None, tm, width), lambda bi, i: (bi, i, 0))
    cls = [pl.BlockSpec((None, tm // d, d * WIDTH_A), lambda bi, i: (bi, i, 0)) for d in DILATIONS]
    return pl.pallas_call(
        _final_kernel,
        grid=(b, t // tm),
        in_specs=[tok(D_MODEL), *cls, *cls, tok(WIDTH_B), tok(WIDTH_M),
                  _const_spec((1, WIDTH_A)), _const_spec((1, WIDTH_B)), _const_spec((1, WIDTH_M)),
                  _const_spec((D_MODEL, D_MODEL)), _const_spec((1, D_MODEL)),
                  _const_spec((D_MODEL, D_FF)), _const_spec((D_FF, D_MODEL))],
        out_specs=tok(D_MODEL),
        out_shape=jax.ShapeDtypeStruct((b, t, D_MODEL), F32),
        scratch_shapes=[pltpu.VMEM((WIDTH_A // LANES, tm, LANES), F32)] * 6,
        compiler_params=_cparams(2),
        name="final",
    )(x, o1, o4, o16, l1, l4, l16, ob, om, ga, gb, gm, wo, gf, w1, w2)


def _rope_tables(t):
    half = ROPE_DIMS // 2
    inv = ROPE_THETA ** (-(np.arange(half, dtype=np.float64) * 2.0 / ROPE_DIMS))
    ang = np.arange(t, dtype=np.float64)[:, None] * inv[None, :]
    cs = jnp.asarray(np.concatenate([np.cos(ang), np.sin(ang)], axis=1), dtype=F32)
    cos, sin = cs[:, :half], cs[:, half:]
    rest = HEAD_DIM - ROPE_DIMS
    cos_h = jnp.concatenate([cos, cos, jnp.ones((t, rest), F32)], axis=1)
    sin_h = jnp.concatenate([-sin, sin, jnp.zeros((t, rest), F32)], axis=1)
    return jnp.tile(cos_h, (1, LANES // HEAD_DIM)), jnp.tile(sin_h, (1, LANES // HEAD_DIM))


def _group_columns(w):
    a, b = WIDTH_A, WIDTH_B
    qa, ka, va = w[:, :a], w[:, a:2 * a], w[:, 2 * a:3 * a]
    qb, kb, vb = w[:, 3 * a:3 * a + b], w[:, 3 * a + b:3 * a + 2 * b], w[:, 3 * a + 2 * b:3 * a + 3 * b]
    return jnp.concatenate([qa, ka, qb, kb, va, vb, w[:, 3 * a + 3 * b:]], axis=1)


def _layer(x, mem, p, tables):
    cos_t, sin_t = tables[x.shape[1]]
    qa, ka, va, (qb, kb, vb, qm) = _proj(x, p["g_attn"], p["w_in"], p["bd"], p["gqka"], p["gqkb"],
                                         p["gqm"], cos_t, sin_t)
    km, vm = _memkv(mem, p["g_mem"], p["w_kv"], p["bd"], p["gkm"])
    pats = [_dilated(qa[n], ka[n], va[n], p["band"], d) for n, d in enumerate(DILATIONS)]
    ob, om = _natten(qb, kb, vb, p["na_bias"], qm, km, vm)
    return _final(x, pats[0][0], pats[1][0], pats[2][0], pats[0][1], pats[1][1], pats[2][1],
                  ob, om, p["ga"], p["gb"], p["gm"], p["w_out"], p["g_ffn"], p["w1"], p["w2"])


def kernel(x_prompt, x_sample, mem_prompt, mem_sample, norm_attn, w_in, q_norm_a, k_norm_a, q_norm_b, k_norm_b, rpb_b, norm_mem, w_mem_kv, q_norm_m, k_norm_m, out_norm_a, out_norm_b, out_norm_m, w_out, norm_ffn, w_ff1, w_ff2):
    depth = w_in.shape[0]
    scale = HEAD_DIM ** -0.5 * LOG2E
    row = lambda v: v.astype(F32)[None, :]
    heads = lambda v, n: jnp.tile(v.astype(F32), n)[None, :]
    idx = np.arange(MXU_TILE)
    bd = jnp.asarray(idx[:, None] // HEAD_DIM == idx[None, :] // HEAD_DIM, dtype=BF16)
    band = _band_bias()
    tables = {t: _rope_tables(t) for t in {x_prompt.shape[1], x_sample.shape[1]}}
    y_prompt, y_sample = x_prompt, x_sample
    for i in range(depth):
        p = dict(
            g_attn=row(norm_attn[i]), w_in=_group_columns(w_in[i]).astype(BF16), bd=bd, band=band,
            gqka=jnp.concatenate([heads(q_norm_a[i], N_HEADS_A) * scale, heads(k_norm_a[i], N_HEADS_A)], axis=1),
            gqkb=jnp.concatenate([heads(q_norm_b[i], N_HEADS_B) * scale, heads(k_norm_b[i], N_HEADS_B)], axis=1),
            gqm=heads(q_norm_m[i], N_HEADS_M) * scale, gkm=heads(k_norm_m[i], N_HEADS_M),
            na_bias=_natten_bias(rpb_b[i]),
            g_mem=row(norm_mem[i]), w_kv=w_mem_kv[i].astype(BF16),
            ga=row(out_norm_a[i]), gb=row(out_norm_b[i]), gm=row(out_norm_m[i]),
            w_out=w_out[i].astype(BF16), g_ffn=row(norm_ffn[i]),
            w1=w_ff1[i].astype(BF16), w2=w_ff2[i].astype(BF16))
        y_prompt = _layer(y_prompt, mem_prompt, p, tables)
        y_sample = _layer(y_sample, mem_sample, p, tables)
    return (y_prompt, y_sample)
```

```python
import functools

import numpy as np
import jax
import jax.numpy as jnp
from jax import lax
from jax.experimental import pallas as pl
from jax.experimental.pallas import tpu as pltpu

D_MODEL = 1024
HEAD_DIM = 64
N_HEADS_A = 6
N_HEADS_B = 6
N_HEADS_M = 4
WIDTH_A = N_HEADS_A * HEAD_DIM
WIDTH_B = N_HEADS_B * HEAD_DIM
WIDTH_M = N_HEADS_M * HEAD_DIM
IN_WIDTH = 3 * WIDTH_A + 3 * WIDTH_B + WIDTH_M
DILATIONS = (1, 4, 16)
DIL_STEP = 4
HALF = 64
GRID_W = 64
NA_ROWS = 8
NA_COLS = 16
N_MEM = 256
D_FF = 4 * D_MODEL
ROPE_THETA = 500000.0
ROPE_DIMS = HEAD_DIM // 4
EPS = 1e-6
NEG = -1e30
LOG2E = 1.4426950408889634
LN2 = 0.6931471805599453

LANES = 128
MXU_TILE = 256
VMEM_LIMIT = 56 * 1024 * 1024

BF16 = jnp.bfloat16
F32 = jnp.float32


def _cparams(n_axes, in_order=False):
    return pltpu.CompilerParams(dimension_semantics=("arbitrary" if in_order else "parallel",) * n_axes,
                                vmem_limit_bytes=VMEM_LIMIT)


def _const_spec(shape):
    nd = len(shape)
    return pl.BlockSpec(shape, lambda *_: (0,) * nd, pipeline_mode=pl.Buffered(1))


def _rms(x, gain):
    return x * lax.rsqrt(jnp.mean(x * x, axis=-1, keepdims=True) + EPS) * gain


def _head_rms(z, bd, gain):
    ss = jnp.dot((z * z).astype(BF16), bd, preferred_element_type=F32)
    return z * lax.rsqrt(ss * (1.0 / HEAD_DIM) + EPS) * gain


def _attend_pair(qp, kp, vaug, bias):
    p, mx = _pair_probs(_pair_scores(qp, kp, bias))
    return _pair_output(p, mx, vaug)


def _pair_scores(qp, kp, bias):
    lane = lax.broadcasted_iota(jnp.int32, qp.shape, 1)
    zero = jnp.zeros_like(qp)
    lhs = jnp.concatenate([jnp.where(lane < HEAD_DIM, qp, zero),
                           jnp.where(lane >= HEAD_DIM, qp, zero)], axis=0)
    s = lax.dot_general(lhs, kp, (((1,), (1,)), ((), ())), preferred_element_type=F32)
    return s if bias is None else s + bias


def _pair_probs(s):
    mx = jnp.max(s, axis=-1, keepdims=True)
    return jnp.exp2(s - mx).astype(BF16), mx


def _pair_output(p, mx, vaug):
    m_rows = p.shape[0] // 2
    ov = jnp.dot(p, vaug, preferred_element_type=F32)
    first = lax.broadcasted_iota(jnp.int32, (m_rows, LANES), 1) < HEAD_DIM
    o = jnp.where(first, ov[:m_rows, :LANES], ov[m_rows:, :LANES])
    l = jnp.where(first, ov[:m_rows, LANES:], ov[m_rows:, LANES:])
    mm = jnp.where(first, mx[:m_rows], mx[m_rows:])
    return o / l, mm * LN2 + jnp.log(l)


def _fill_vaug(vaug_ref, row0, v, ones=True):
    rows = v.shape[0]
    for hp in range(v.shape[1] // LANES):
        vaug_ref[row0:row0 + rows, 2 * LANES * hp:2 * LANES * hp + LANES] = v[:, LANES * hp:LANES * (hp + 1)]
        if ones:
            vaug_ref[row0:row0 + rows, 2 * LANES * hp + LANES:2 * LANES * (hp + 1)] = jnp.ones((rows, LANES), BF16)


def _fill_ones(vaug_ref):
    for hp in range(vaug_ref.shape[1] // (2 * LANES)):
        vaug_ref[:, 2 * LANES * hp + LANES:2 * LANES * (hp + 1)] = jnp.ones((vaug_ref.shape[0], LANES), BF16)


def _first_step(n_axes):
    first = pl.program_id(0) == 0
    for axis in range(1, n_axes):
        first = first & (pl.program_id(axis) == 0)
    return first


PROJ_TM = 1024


def _proj_kernel(x_ref, g_ref, w_ref, bd_ref, gqka_ref, gqkb_ref, gqm_ref, cos_ref, sin_ref, *refs):
    n_dil = len(DILATIONS)
    qa_refs, ka_refs, va_refs = refs[:n_dil], refs[n_dil:2 * n_dil], refs[2 * n_dil:3 * n_dil]
    qb_ref, kb_ref, vb_ref, qm_ref = refs[3 * n_dil:3 * n_dil + 4]
    stages = refs[3 * n_dil + 4:]
    tm = x_ref.shape[0]
    x = x_ref[...]
    xn = _rms(x, g_ref[...]).astype(BF16)
    bd = bd_ref[...]

    def project(c0, width):
        return jnp.dot(xn, w_ref[:, c0:c0 + width], preferred_element_type=F32)

    def normed(z, gain_ref):
        return jnp.concatenate(
            [_head_rms(z[:, off:off + MXU_TILE], bd, gain_ref[:, off:off + MXU_TILE])
             for off in range(0, z.shape[1], MXU_TILE)], axis=1)

    def rope(y):
        cos = cos_ref[...]
        sin = sin_ref[...]
        lane = lax.broadcasted_iota(jnp.int32, cos.shape, 1) % HEAD_DIM
        low = lane < ROPE_DIMS // 2
        parts = []
        for c in range(0, y.shape[1], LANES):
            yc = y[:, c:c + LANES]
            partner = jnp.where(low, pltpu.roll(yc, LANES - ROPE_DIMS // 2, axis=1),
                                pltpu.roll(yc, ROPE_DIMS // 2, axis=1))
            parts.append(yc * cos + partner * sin)
        return jnp.concatenate(parts, axis=1)

    def emit_classes(y, out_refs, stage_a, stage_b):
        n_tiles = WIDTH_A // LANES
        rows4, rows16 = tm // DIL_STEP, tm // (DIL_STEP * DIL_STEP)

        def put(ref, r, c, val):
            col = (r * n_tiles + c) * LANES
            ref[:, col:col + LANES] = val.astype(BF16)

        for c in range(n_tiles):
            put(out_refs[0], 0, c, y[:, c * LANES:(c + 1) * LANES])
            stage_a[c] = y[:, c * LANES:(c + 1) * LANES]
        for r4 in range(DIL_STEP):
            for c in range(n_tiles):
                cls = stage_a[c, pl.ds(r4, rows4, stride=DIL_STEP), :]
                put(out_refs[1], r4, c, cls)
                stage_b[c, r4 * rows4:(r4 + 1) * rows4, :] = cls
        for r4 in range(DIL_STEP):
            for q in range(DIL_STEP):
                for c in range(n_tiles):
                    put(out_refs[2], r4 + DIL_STEP * q, c,
                        stage_b[c, pl.ds(r4 * rows4 + q, rows16, stride=DIL_STEP), :])

    wa, wb = WIDTH_A, WIDTH_B
    p_a = project(0, 2 * wa)
    p_b = project(2 * wa, 2 * wb)
    qk_a = rope(normed(p_a, gqka_ref))
    emit_classes(qk_a[:, :wa], qa_refs, stages[0], stages[1])
    emit_classes(qk_a[:, wa:], ka_refs, stages[2], stages[3])
    p_v = project(2 * wa + 2 * wb, wa + wb)
    qk_b = normed(p_b, gqkb_ref).astype(BF16)
    qb_ref[...] = qk_b[:, :wb]
    kb_ref[...] = qk_b[:, wb:]
    p_m = project(3 * wa + 3 * wb, WIDTH_M)
    emit_classes(p_v[:, :wa], va_refs, stages[4], stages[5])
    vb_ref[...] = p_v[:, wa:].astype(BF16)
    qm_ref[...] = normed(p_m, gqm_ref).astype(BF16)


def _proj(x, g, w_in, bd, gqka, gqkb, gqm, cos_t, sin_t):
    b, t, _ = x.shape
    tm = PROJ_TM
    tok = lambda width: pl.BlockSpec((None, tm, width), lambda bi, i: (bi, i, 0))
    cls = lambda dil, w: pl.BlockSpec((None, tm // dil, dil * w), lambda bi, i: (bi, i, 0))
    cls_shape = lambda dil, w: jax.ShapeDtypeStruct((b, t // dil, dil * w), BF16)
    cls_widths = (WIDTH_A, WIDTH_A, WIDTH_A)
    rest_widths = (WIDTH_B, WIDTH_B, WIDTH_B, WIDTH_M)
    n = len(DILATIONS)
    outs = pl.pallas_call(
        _proj_kernel,
        grid=(b, t // tm),
        in_specs=[tok(D_MODEL), _const_spec((1, D_MODEL)), _const_spec((D_MODEL, IN_WIDTH)),
                  _const_spec((MXU_TILE, MXU_TILE)),
                  _const_spec((1, 2 * WIDTH_A)), _const_spec((1, 2 * WIDTH_B)), _const_spec((1, WIDTH_M)),
                  pl.BlockSpec((tm, LANES), lambda bi, i: (i, 0)),
                  pl.BlockSpec((tm, LANES), lambda bi, i: (i, 0))],
        out_specs=[cls(d, w) for w in cls_widths for d in DILATIONS] + [tok(w) for w in rest_widths],
        out_shape=[cls_shape(d, w) for w in cls_widths for d in DILATIONS]
                  + [jax.ShapeDtypeStruct((b, t, w), BF16) for w in rest_widths],
        scratch_shapes=[pltpu.VMEM((WIDTH_A // LANES, tm, LANES), F32)] * 6,
        compiler_params=_cparams(2),
        name="proj",
    )(x, g, w_in, bd, gqka, gqkb, gqm, cos_t, sin_t)
    return outs[:n], outs[n:2 * n], outs[2 * n:3 * n], outs[3 * n:]


def _memkv_kernel(mem_ref, g_ref, w_ref, bd_ref, gk_ref, km_ref, vaug_ref):
    mn = _rms(mem_ref[...], g_ref[...]).astype(BF16)
    kv = jnp.dot(mn, w_ref[...], preferred_element_type=F32)
    km_ref[...] = _head_rms(kv[:, :WIDTH_M], bd_ref[...], gk_ref[...]).astype(BF16)
    _fill_vaug(vaug_ref, 0, kv[:, WIDTH_M:].astype(BF16))


def _memkv(mem, g, w_kv, bd, gk):
    b = mem.shape[0]
    return pl.pallas_call(
        _memkv_kernel,
        grid=(b,),
        in_specs=[pl.BlockSpec((None, N_MEM, D_MODEL), lambda bi: (bi, 0, 0)),
                  _const_spec((1, D_MODEL)), _const_spec((D_MODEL, 2 * WIDTH_M)),
                  _const_spec((MXU_TILE, MXU_TILE)), _const_spec((1, WIDTH_M))],
        out_specs=[pl.BlockSpec((None, N_MEM, WIDTH_M), lambda bi: (bi, 0, 0)),
                   pl.BlockSpec((None, N_MEM, 2 * WIDTH_M), lambda bi: (bi, 0, 0))],
        out_shape=[jax.ShapeDtypeStruct((b, N_MEM, WIDTH_M), BF16),
                   jax.ShapeDtypeStruct((b, N_MEM, 2 * WIDTH_M), BF16)],
        compiler_params=_cparams(1),
        name="memkv",
    )(mem, g, w_kv, bd, gk)


DIL_QB = 2048
DIL_SUB = 2 * HALF


def _dilated_kernel(q_ref, kp_ref, kc_ref, kn_ref, vp_ref, vc_ref, vn_ref, bias_ref,
                    o_ref, lse_ref, kbuf, vaug, *, n_sub_total):
    qb = q_ref.shape[0]
    kbuf[0:HALF] = kp_ref[...]
    kbuf[HALF:HALF + qb] = kc_ref[...]
    kbuf[HALF + qb:2 * HALF + qb] = kn_ref[...]
    _fill_vaug(vaug, 0, vp_ref[...])
    _fill_vaug(vaug, HALF, vc_ref[...])
    _fill_vaug(vaug, HALF + qb, vn_ref[...])
    n_sub = qb // DIL_SUB
    first_sub = pl.program_id(2) * n_sub

    for j in range(n_sub):
        gs = first_sub + j
        variant = jnp.where(gs == 0, 0, jnp.where(gs == n_sub_total - 1, 2, 1))
        r0 = j * DIL_SUB
        for hp in range(q_ref.shape[1] // LANES):
            o, lse = _attend_pair(q_ref[r0:r0 + DIL_SUB, LANES * hp:LANES * (hp + 1)],
                                  kbuf[r0:r0 + 2 * DIL_SUB, LANES * hp:LANES * (hp + 1)],
                                  vaug[r0:r0 + 2 * DIL_SUB, 2 * LANES * hp:2 * LANES * (hp + 1)],
                                  bias_ref[variant])
            o_ref[r0:r0 + DIL_SUB, LANES * hp:LANES * (hp + 1)] = o.astype(BF16)
            lse_ref[r0:r0 + DIL_SUB, LANES * hp:LANES * (hp + 1)] = lse


def _dilated(q, k, v, bias, dil):
    b, ln = q.shape[0], q.shape[1]
    qb = min(DIL_QB, ln)
    classes = min(dil, DIL_QB // qb)
    w = classes * (q.shape[2] // dil)
    hb = qb // HALF
    n_halo = ln // HALF
    main = pl.BlockSpec((None, qb, w), lambda bi, r, i: (bi, i, r))
    prev = pl.BlockSpec((None, HALF, w), lambda bi, r, i: (bi, jnp.maximum(i * hb - 1, 0), r))
    nxt = pl.BlockSpec((None, HALF, w), lambda bi, r, i: (bi, jnp.minimum((i + 1) * hb, n_halo - 1), r))
    o, lse = pl.pallas_call(
        functools.partial(_dilated_kernel, n_sub_total=ln // DIL_SUB),
        grid=(b, dil // classes, ln // qb),
        in_specs=[main, prev, main, nxt, prev, main, nxt, _const_spec(bias.shape)],
        out_specs=[main, main],
        out_shape=[jax.ShapeDtypeStruct(q.shape, BF16), jax.ShapeDtypeStruct(q.shape, F32)],
        scratch_shapes=[pltpu.VMEM((qb + 2 * HALF, w), BF16),
                        pltpu.VMEM((qb + 2 * HALF, 2 * w), BF16)],
        compiler_params=_cparams(3),
        name=f"dilated{dil}",
    )(q, k, k, k, v, v, v, bias)
    return o, lse


def _band_bias():
    row = np.arange(DIL_SUB)[:, None]
    col = np.arange(2 * DIL_SUB)[None, :]
    band = (col - row >= 0) & (col - row <= 2 * HALF)
    variants = [band & (col >= HALF), band, band & (col < 2 * DIL_SUB - HALF)]
    tab = np.stack([np.where(np.concatenate([m, m], axis=0), 0.0, NEG) for m in variants])
    return jnp.asarray(tab, dtype=F32)


NA_GROUP = 32
NA_TOK = NA_GROUP * GRID_W
NA_KEYS = NA_ROWS * GRID_W


def _natten_kernel(q_ref, kp_ref, kc_ref, kn_ref, vp_ref, vc_ref, vn_ref, bias_ref, qm_ref, km_ref, vm_ref,
                   o_ref, om_ref, kbuf, vaug, *, n_rows):
    kbuf[0:NA_TOK] = kp_ref[...]
    kbuf[NA_TOK:2 * NA_TOK] = kc_ref[...]
    kbuf[2 * NA_TOK:3 * NA_TOK] = kn_ref[...]
    pl.when(_first_step(2))(lambda: _fill_ones(vaug))
    _fill_vaug(vaug, 0, vp_ref[...], ones=False)
    _fill_vaug(vaug, NA_TOK, vc_ref[...], ones=False)
    _fill_vaug(vaug, 2 * NA_TOK, vn_ref[...], ones=False)
    g = pl.program_id(1)

    for j in range(NA_GROUP):
        r = g * NA_GROUP + j
        r0 = jnp.clip(r - NA_ROWS // 2, 0, n_rows - NA_ROWS)
        off = r - r0
        start = pl.multiple_of((r0 - (g - 1) * NA_GROUP) * GRID_W, GRID_W)
        q0 = j * GRID_W
        for hp in range(WIDTH_B // LANES):
            o, _ = _attend_pair(q_ref[q0:q0 + GRID_W, LANES * hp:LANES * (hp + 1)],
                                kbuf[pl.ds(start, NA_KEYS), LANES * hp:LANES * (hp + 1)],
                                vaug[pl.ds(start, NA_KEYS), 2 * LANES * hp:2 * LANES * (hp + 1)],
                                bias_ref[off, hp])
            o_ref[q0:q0 + GRID_W, LANES * hp:LANES * (hp + 1)] = o.astype(BF16)
        if j % 2 == 1:
            m0 = (j - 1) * GRID_W
            for hp in range(WIDTH_M // LANES):
                o, _ = _attend_pair(qm_ref[m0:m0 + 2 * GRID_W, LANES * hp:LANES * (hp + 1)],
                                    km_ref[:, LANES * hp:LANES * (hp + 1)],
                                    vm_ref[:, 2 * LANES * hp:2 * LANES * (hp + 1)], None)
                om_ref[m0:m0 + 2 * GRID_W, LANES * hp:LANES * (hp + 1)] = o.astype(BF16)


def _natten(q, k, v, bias, qm, km, vm):
    b, t, w = q.shape
    n_rows = t // GRID_W
    n_groups = n_rows // NA_GROUP
    blk = lambda f: pl.BlockSpec((None, NA_TOK, w), lambda bi, g: (bi, f(g), 0))
    cur = blk(lambda g: g)
    prev = blk(lambda g: jnp.maximum(g - 1, 0))
    nxt = blk(lambda g: jnp.minimum(g + 1, n_groups - 1))
    cur_m = pl.BlockSpec((None, NA_TOK, WIDTH_M), lambda bi, g: (bi, g, 0))
    per_b = lambda width: pl.BlockSpec((None, N_MEM, width), lambda bi, g: (bi, 0, 0))
    return pl.pallas_call(
        functools.partial(_natten_kernel, n_rows=n_rows),
        grid=(b, n_groups),
        in_specs=[cur, prev, cur, nxt, prev, cur, nxt, _const_spec(bias.shape),
                  cur_m, per_b(WIDTH_M), per_b(2 * WIDTH_M)],
        out_specs=[cur, cur_m],
        out_shape=[jax.ShapeDtypeStruct((b, t, w), BF16), jax.ShapeDtypeStruct((b, t, WIDTH_M), BF16)],
        scratch_shapes=[pltpu.VMEM((3 * NA_TOK, w), BF16), pltpu.VMEM((3 * NA_TOK, 2 * w), BF16)],
        compiler_params=_cparams(2, in_order=True),
        name="natten",
    )(q, k, k, k, v, v, v, bias, qm, km, vm)


def _natten_bias(rpb):
    c = np.arange(GRID_W)[:, None]
    kc = np.arange(GRID_W)[None, :]
    c0 = np.clip(c - NA_COLS // 2, 0, GRID_W - NA_COLS)
    valid = (kc >= c0) & (kc < c0 + NA_COLS)
    dcol = kc - c + NA_COLS - 1
    col_sel = (valid[..., None] & (dcol[..., None] == np.arange(2 * NA_COLS - 1))).astype(np.float32)
    cols = jnp.einsum("hrd,ckd->hcrk", rpb.astype(F32), col_sel, precision=lax.Precision.HIGHEST)
    cols = jnp.where(valid[None, :, None, :], cols * LOG2E, NEG)
    cols = cols.reshape(N_HEADS_B // 2, 2 * GRID_W, (2 * NA_ROWS - 1) * GRID_W)
    tab = jnp.stack([cols[:, :, (NA_ROWS - 1 - off) * GRID_W:(2 * NA_ROWS - 1 - off) * GRID_W]
                     for off in range(NA_ROWS)])
    return tab


FIN_TM = 512
FF_CHUNK = 1024


def _final_kernel(x_ref, o1_ref, o4_ref, o16_ref, l1_ref, l4_ref, l16_ref, ob_ref, om_ref,
                  ga_ref, gb_ref, gm_ref, wo_ref, gf_ref, w1_ref, w2_ref, y_ref,
                  *stages):
    tm = x_ref.shape[0]
    n_tiles = WIDTH_A // LANES
    rows4, rows16 = tm // DIL_STEP, tm // (DIL_STEP * DIL_STEP)

    def token_order(ref, stage, stage_b=None):
        src = lambda col: ref[:, col:col + LANES].astype(F32)
        if stage_b is not None:
            for r4 in range(DIL_STEP):
                for q in range(DIL_STEP):
                    for c in range(n_tiles):
                        stage_b[c, pl.ds(r4 * rows4 + q, rows16, stride=DIL_STEP), :] = (
                            src((r4 + DIL_STEP * q) * WIDTH_A + c * LANES))
        for r4 in range(DIL_STEP):
            for c in range(n_tiles):
                if stage_b is not None:
                    cls = stage_b[c, r4 * rows4:(r4 + 1) * rows4, :]
                else:
                    cls = src(r4 * WIDTH_A + c * LANES)
                stage[c, pl.ds(r4, rows4, stride=DIL_STEP), :] = cls
        return jnp.concatenate([stage[c] for c in range(n_tiles)], axis=1)

    o1, l1 = o1_ref[...].astype(F32), l1_ref[...]
    o4 = token_order(o4_ref, stages[0])
    l4 = token_order(l4_ref, stages[1])
    o16 = token_order(o16_ref, stages[2], stages[4])
    l16 = token_order(l16_ref, stages[3], stages[5])
    mx = jnp.maximum(jnp.maximum(l1, l4), l16)
    e1, e4, e16 = jnp.exp(l1 - mx), jnp.exp(l4 - mx), jnp.exp(l16 - mx)
    oa = (e1 * o1 + e4 * o4 + e16 * o16) / (e1 + e4 + e16)
    mixed = jnp.concatenate([_rms(oa, ga_ref[...]), _rms(ob_ref[...].astype(F32), gb_ref[...]),
                             _rms(om_ref[...].astype(F32), gm_ref[...])], axis=1).astype(BF16)
    x1 = x_ref[...] + jnp.dot(mixed, wo_ref[...], preferred_element_type=F32)
    hf = _rms(x1, gf_ref[...]).astype(BF16)
    acc = x1
    for c in range(0, D_FF, FF_CHUNK):
        h = jnp.dot(hf, w1_ref[:, c:c + FF_CHUNK], preferred_element_type=F32)
        h = jnp.square(jnp.maximum(h, 0.0)).astype(BF16)
        acc = acc + jnp.dot(h, w2_ref[c:c + FF_CHUNK, :], preferred_element_type=F32)
    y_ref[...] = acc


def _final(x, o1, o4, o16, l1, l4, l16, ob, om, ga, gb, gm, wo, gf, w1, w2):
    b, t, _ = x.shape
    tm = FIN_TM
    tok = lambda width: pl.BlockSpec((None, tm, width), lambda bi, i: (bi, i, 0))
    cls = [pl.BlockSpec((None, tm // d, d * WIDTH_A), lambda bi, i: (bi, i, 0)) for d in DILATIONS]
    return pl.pallas_call(
        _final_kernel,
        grid=(b, t // tm),
        in_specs=[tok(D_MODEL), *cls, *cls, tok(WIDTH_B), tok(WIDTH_M),
                  _const_spec((1, WIDTH_A)), _const_spec((1, WIDTH_B)), _const_spec((1, WIDTH_M)),
                  _const_spec((D_MODEL, D_MODEL)), _const_spec((1, D_MODEL)),
                  _const_spec((D_MODEL, D_FF)), _const_spec((D_FF, D_MODEL))],
        out_specs=tok(D_MODEL),
        out_shape=jax.ShapeDtypeStruct((b, t, D_MODEL), F32),
        scratch_shapes=[pltpu.VMEM((WIDTH_A // LANES, tm, LANES), F32)] * 6,
        compiler_params=_cparams(2),
        name="final",
    )(x, o1, o4, o16, l1, l4, l16, ob, om, ga, gb, gm, wo, gf, w1, w2)


def _rope_tables(t):
    half = ROPE_DIMS // 2
    inv = ROPE_THETA ** (-(np.arange(half, dtype=np.float64) * 2.0 / ROPE_DIMS))
    ang = np.arange(t, dtype=np.float64)[:, None] * inv[None, :]
    cs = jnp.asarray(np.concatenate([np.cos(ang), np.sin(ang)], axis=1), dtype=F32)
    cos, sin = cs[:, :half], cs[:, half:]
    rest = HEAD_DIM - ROPE_DIMS
    cos_h = jnp.concatenate([cos, cos, jnp.ones((t, rest), F32)], axis=1)
    sin_h = jnp.concatenate([-sin, sin, jnp.zeros((t, rest), F32)], axis=1)
    return jnp.tile(cos_h, (1, LANES // HEAD_DIM)), jnp.tile(sin_h, (1, LANES // HEAD_DIM))


def _group_columns(w):
    a, b = WIDTH_A, WIDTH_B
    qa, ka, va = w[:, :a], w[:, a:2 * a], w[:, 2 * a:3 * a]
    qb, kb, vb = w[:, 3 * a:3 * a + b], w[:, 3 * a + b:3 * a + 2 * b], w[:, 3 * a + 2 * b:3 * a + 3 * b]
    return jnp.concatenate([qa, ka, qb, kb, va, vb, w[:, 3 * a + 3 * b:]], axis=1)


def _layer(x, mem, p, tables):
    cos_t, sin_t = tables[x.shape[1]]
    qa, ka, va, (qb, kb, vb, qm) = _proj(x, p["g_attn"], p["w_in"], p["bd"], p["gqka"], p["gqkb"],
                                         p["gqm"], cos_t, sin_t)
    km, vm = _memkv(mem, p["g_mem"], p["w_kv"], p["bd"], p["gkm"])
    pats = [_dilated(qa[n], ka[n], va[n], p["band"], d) for n, d in enumerate(DILATIONS)]
    ob, om = _natten(qb, kb, vb, p["na_bias"], qm, km, vm)
    return _final(x, pats[0][0], pats[1][0], pats[2][0], pats[0][1], pats[1][1], pats[2][1],
                  ob, om, p["ga"], p["gb"], p["gm"], p["w_out"], p["g_ffn"], p["w1"], p["w2"])


def kernel(x_prompt, x_sample, mem_prompt, mem_sample, norm_attn, w_in, q_norm_a, k_norm_a, q_norm_b, k_norm_b, rpb_b, norm_mem, w_mem_kv, q_norm_m, k_norm_m, out_norm_a, out_norm_b, out_norm_m, w_out, norm_ffn, w_ff1, w_ff2):
    depth = w_in.shape[0]
    scale = HEAD_DIM ** -0.5 * LOG2E
    row = lambda v: v.astype(F32)[None, :]
    heads = lambda v, n: jnp.tile(v.astype(F32), n)[None, :]
    idx = np.arange(MXU_TILE)
    bd = jnp.asarray(idx[:, None] // HEAD_DIM == idx[None, :] // HEAD_DIM, dtype=BF16)
    band = _band_bias()
    tables = {t: _rope_tables(t) for t in {x_prompt.shape[1], x_sample.shape[1]}}
    y_prompt, y_sample = x_prompt, x_sample
    for i in range(depth):
        p = dict(
            g_attn=row(norm_attn[i]), w_in=_group_columns(w_in[i]).astype(BF16), bd=bd, band=band,
            gqka=jnp.concatenate([heads(q_norm_a[i], N_HEADS_A) * scale, heads(k_norm_a[i], N_HEADS_A)], axis=1),
            gqkb=jnp.concatenate([heads(q_norm_b[i], N_HEADS_B) * scale, heads(k_norm_b[i], N_HEADS_B)], axis=1),
            gqm=heads(q_norm_m[i], N_HEADS_M) * scale, gkm=heads(k_norm_m[i], N_HEADS_M),
            na_bias=_natten_bias(rpb_b[i]),
            g_mem=row(norm_mem[i]), w_kv=w_mem_kv[i].astype(BF16),
            ga=row(out_norm_a[i]), gb=row(out_norm_b[i]), gm=row(out_norm_m[i]),
            w_out=w_out[i].astype(BF16), g_ffn=row(norm_ffn[i]),
            w1=w_ff1[i].astype(BF16), w2=w_ff2[i].astype(BF16))
        y_prompt = _layer(y_prompt, mem_prompt, p, tables)
        y_sample = _layer(y_sample, mem_sample, p, tables)
    return (y_prompt, y_sample)
```

```python
import functools

import numpy as np
import jax
import jax.numpy as jnp
from jax import lax
from jax.experimental import pallas as pl
from jax.experimental.pallas import tpu as pltpu

D_MODEL = 1024
HEAD_DIM = 64
N_HEADS_A = 6
N_HEADS_B = 6
N_HEADS_M = 4
WIDTH_A = N_HEADS_A * HEAD_DIM
WIDTH_B = N_HEADS_B * HEAD_DIM
WIDTH_M = N_HEADS_M * HEAD_DIM
IN_WIDTH = 3 * WIDTH_A + 3 * WIDTH_B + WIDTH_M
DILATIONS = (1, 4, 16)
DIL_STEP = 4
HALF = 64
GRID_W = 64
NA_ROWS = 8
NA_COLS = 16
N_MEM = 256
D_FF = 4 * D_MODEL
ROPE_THETA = 500000.0
ROPE_DIMS = HEAD_DIM // 4
EPS = 1e-6
NEG = -1e30
LOG2E = 1.4426950408889634
LN2 = 0.6931471805599453

LANES = 128
MXU_TILE = 256
VMEM_LIMIT = 56 * 1024 * 1024

BF16 = jnp.bfloat16
F32 = jnp.float32


def _cparams(n_axes, in_order=False):
    return pltpu.CompilerParams(dimension_semantics=("arbitrary" if in_order else "parallel",) * n_axes,
                                vmem_limit_bytes=VMEM_LIMIT)


def _const_spec(shape):
    nd = len(shape)
    return pl.BlockSpec(shape, lambda *_: (0,) * nd, pipeline_mode=pl.Buffered(1))


def _rms(x, gain):
    return x * lax.rsqrt(jnp.mean(x * x, axis=-1, keepdims=True) + EPS) * gain


def _head_rms(z, bd, gain):
    ss = jnp.dot((z * z).astype(BF16), bd, preferred_element_type=F32)
    return z * lax.rsqrt(ss * (1.0 / HEAD_DIM) + EPS) * gain


def _attend_pair(qp, kp, vaug, bias):
    s = _pair_scores(qp, kp, bias)
    m_rows = qp.shape[0]
    halves = [_pair_probs(s[h * m_rows:(h + 1) * m_rows]) for h in range(2)]
    ovs = [jnp.dot(p, vaug, preferred_element_type=F32) for p, _ in halves]
    first = lax.broadcasted_iota(jnp.int32, (m_rows, LANES), 1) < HEAD_DIM
    o = jnp.where(first, ovs[0][:, :LANES], ovs[1][:, :LANES])
    l = jnp.where(first, ovs[0][:, LANES:], ovs[1][:, LANES:])
    mm = jnp.where(first, halves[0][1], halves[1][1])
    return o / l, mm * LN2 + jnp.log(l)


def _pair_scores(qp, kp, bias):
    lane = lax.broadcasted_iota(jnp.int32, qp.shape, 1)
    zero = jnp.zeros_like(qp)
    lhs = jnp.concatenate([jnp.where(lane < HEAD_DIM, qp, zero),
                           jnp.where(lane >= HEAD_DIM, qp, zero)], axis=0)
    s = lax.dot_general(lhs, kp, (((1,), (1,)), ((), ())), preferred_element_type=F32)
    return s if bias is None else s + bias


def _pair_probs(s):
    mx = jnp.max(s, axis=-1, keepdims=True)
    return jnp.exp2(s - mx).astype(BF16), mx


def _pair_output(p, mx, vaug):
    m_rows = p.shape[0] // 2
    ov = jnp.dot(p, vaug, preferred_element_type=F32)
    first = lax.broadcasted_iota(jnp.int32, (m_rows, LANES), 1) < HEAD_DIM
    o = jnp.where(first, ov[:m_rows, :LANES], ov[m_rows:, :LANES])
    l = jnp.where(first, ov[:m_rows, LANES:], ov[m_rows:, LANES:])
    mm = jnp.where(first, mx[:m_rows], mx[m_rows:])
    return o / l, mm * LN2 + jnp.log(l)


def _fill_vaug(vaug_ref, row0, v, ones=True):
    rows = v.shape[0]
    for hp in range(v.shape[1] // LANES):
        vaug_ref[row0:row0 + rows, 2 * LANES * hp:2 * LANES * hp + LANES] = v[:, LANES * hp:LANES * (hp + 1)]
        if ones:
            vaug_ref[row0:row0 + rows, 2 * LANES * hp + LANES:2 * LANES * (hp + 1)] = jnp.ones((rows, LANES), BF16)


def _fill_ones(vaug_ref):
    for hp in range(vaug_ref.shape[1] // (2 * LANES)):
        vaug_ref[:, 2 * LANES * hp + LANES:2 * LANES * (hp + 1)] = jnp.ones((vaug_ref.shape[0], LANES), BF16)


def _first_step(n_axes):
    first = pl.program_id(0) == 0
    for axis in range(1, n_axes):
        first = first & (pl.program_id(axis) == 0)
    return first


PROJ_TM = 1024


def _proj_kernel(x_ref, g_ref, w_ref, bd_ref, gqka_ref, gqkb_ref, gqm_ref, cos_ref, sin_ref, *refs):
    n_dil = len(DILATIONS)
    qa_refs, ka_refs, va_refs = refs[:n_dil], refs[n_dil:2 * n_dil], refs[2 * n_dil:3 * n_dil]
    qb_ref, kb_ref, vb_ref, qm_ref = refs[3 * n_dil:3 * n_dil + 4]
    stages = refs[3 * n_dil + 4:]
    tm = x_ref.shape[0]
    x = x_ref[...]
    xn = _rms(x, g_ref[...]).astype(BF16)
    bd = bd_ref[...]

    def project(c0, width):
        return jnp.dot(xn, w_ref[:, c0:c0 + width], preferred_element_type=F32)

    def normed(z, gain_ref):
        return jnp.concatenate(
            [_head_rms(z[:, off:off + MXU_TILE], bd, gain_ref[:, off:off + MXU_TILE])
             for off in range(0, z.shape[1], MXU_TILE)], axis=1)

    def rope(y):
        cos = cos_ref[...]
        sin = sin_ref[...]
        lane = lax.broadcasted_iota(jnp.int32, cos.shape, 1) % HEAD_DIM
        low = lane < ROPE_DIMS // 2
        parts = []
        for c in range(0, y.shape[1], LANES):
            yc = y[:, c:c + LANES]
            partner = jnp.where(low, pltpu.roll(yc, LANES - ROPE_DIMS // 2, axis=1),
                                pltpu.roll(yc, ROPE_DIMS // 2, axis=1))
            parts.append(yc * cos + partner * sin)
        return jnp.concatenate(parts, axis=1)

    def emit_classes(y, out_refs, stage_a, stage_b):
        n_tiles = WIDTH_A // LANES
        rows4, rows16 = tm // DIL_STEP, tm // (DIL_STEP * DIL_STEP)

        def put(ref, r, c, val):
            col = (r * n_tiles + c) * LANES
            ref[:, col:col + LANES] = val.astype(BF16)

        for c in range(n_tiles):
            put(out_refs[0], 0, c, y[:, c * LANES:(c + 1) * LANES])
            stage_a[c] = y[:, c * LANES:(c + 1) * LANES]
        for r4 in range(DIL_STEP):
            for c in range(n_tiles):
                cls = stage_a[c, pl.ds(r4, rows4, stride=DIL_STEP), :]
                put(out_refs[1], r4, c, cls)
                stage_b[c, r4 * rows4:(r4 + 1) * rows4, :] = cls
        for r4 in range(DIL_STEP):
            for q in range(DIL_STEP):
                for c in range(n_tiles):
                    put(out_refs[2], r4 + DIL_STEP * q, c,
                        stage_b[c, pl.ds(r4 * rows4 + q, rows16, stride=DIL_STEP), :])

    wa, wb = WIDTH_A, WIDTH_B
    p_a = project(0, 2 * wa)
    p_b = project(2 * wa, 2 * wb)
    qk_a = rope(normed(p_a, gqka_ref))
    emit_classes(qk_a[:, :wa], qa_refs, stages[0], stages[1])
    emit_classes(qk_a[:, wa:], ka_refs, stages[2], stages[3])
    p_v = project(2 * wa + 2 * wb, wa + wb)
    qk_b = normed(p_b, gqkb_ref).astype(BF16)
    qb_ref[...] = qk_b[:, :wb]
    kb_ref[...] = qk_b[:, wb:]
    p_m = project(3 * wa + 3 * wb, WIDTH_M)
    emit_classes(p_v[:, :wa], va_refs, stages[4], stages[5])
    vb_ref[...] = p_v[:, wa:].astype(BF16)
    qm_ref[...] = normed(p_m, gqm_ref).astype(BF16)


def _proj(x, g, w_in, bd, gqka, gqkb, gqm, cos_t, sin_t):
    b, t, _ = x.shape
    tm = PROJ_TM
    tok = lambda width: pl.BlockSpec((None, tm, width), lambda bi, i: (bi, i, 0))
    cls = lambda dil, w: pl.BlockSpec((None, tm // dil, dil * w), lambda bi, i: (bi, i, 0))
    cls_shape = lambda dil, w: jax.ShapeDtypeStruct((b, t // dil, dil * w), BF16)
    cls_widths = (WIDTH_A, WIDTH_A, WIDTH_A)
    rest_widths = (WIDTH_B, WIDTH_B, WIDTH_B, WIDTH_M)
    n = len(DILATIONS)
    outs = pl.pallas_call(
        _proj_kernel,
        grid=(b, t // tm),
        in_specs=[tok(D_MODEL), _const_spec((1, D_MODEL)), _const_spec((D_MODEL, IN_WIDTH)),
                  _const_spec((MXU_TILE, MXU_TILE)),
                  _const_spec((1, 2 * WIDTH_A)), _const_spec((1, 2 * WIDTH_B)), _const_spec((1, WIDTH_M)),
                  pl.BlockSpec((tm, LANES), lambda bi, i: (i, 0)),
                  pl.BlockSpec((tm, LANES), lambda bi, i: (i, 0))],
        out_specs=[cls(d, w) for w in cls_widths for d in DILATIONS] + [tok(w) for w in rest_widths],
        out_shape=[cls_shape(d, w) for w in cls_widths for d in DILATIONS]
                  + [jax.ShapeDtypeStruct((b, t, w), BF16) for w in rest_widths],
        scratch_shapes=[pltpu.VMEM((WIDTH_A // LANES, tm, LANES), F32)] * 6,
        compiler_params=_cparams(2),
        name="proj",
    )(x, g, w_in, bd, gqka, gqkb, gqm, cos_t, sin_t)
    return outs[:n], outs[n:2 * n], outs[2 * n:3 * n], outs[3 * n:]


def _memkv_kernel(mem_ref, g_ref, w_ref, bd_ref, gk_ref, km_ref, vaug_ref):
    mn = _rms(mem_ref[...], g_ref[...]).astype(BF16)
    kv = jnp.dot(mn, w_ref[...], preferred_element_type=F32)
    km_ref[...] = _head_rms(kv[:, :WIDTH_M], bd_ref[...], gk_ref[...]).astype(BF16)
    _fill_vaug(vaug_ref, 0, kv[:, WIDTH_M:].astype(BF16))


def _memkv(mem, g, w_kv, bd, gk):
    b = mem.shape[0]
    return pl.pallas_call(
        _memkv_kernel,
        grid=(b,),
        in_specs=[pl.BlockSpec((None, N_MEM, D_MODEL), lambda bi: (bi, 0, 0)),
                  _const_spec((1, D_MODEL)), _const_spec((D_MODEL, 2 * WIDTH_M)),
                  _const_spec((MXU_TILE, MXU_TILE)), _const_spec((1, WIDTH_M))],
        out_specs=[pl.BlockSpec((None, N_MEM, WIDTH_M), lambda bi: (bi, 0, 0)),
                   pl.BlockSpec((None, N_MEM, 2 * WIDTH_M), lambda bi: (bi, 0, 0))],
        out_shape=[jax.ShapeDtypeStruct((b, N_MEM, WIDTH_M), BF16),
                   jax.ShapeDtypeStruct((b, N_MEM, 2 * WIDTH_M), BF16)],
        compiler_params=_cparams(1),
        name="memkv",
    )(mem, g, w_kv, bd, gk)


DIL_QB = 2048
DIL_SUB = 2 * HALF


def _dilated_kernel(q_ref, kp_ref, kc_ref, kn_ref, vp_ref, vc_ref, vn_ref, bias_ref,
                    o_ref, lse_ref, kbuf, vaug, *, n_sub_total):
    qb = q_ref.shape[0]
    kbuf[0:HALF] = kp_ref[...]
    kbuf[HALF:HALF + qb] = kc_ref[...]
    kbuf[HALF + qb:2 * HALF + qb] = kn_ref[...]
    _fill_vaug(vaug, 0, vp_ref[...])
    _fill_vaug(vaug, HALF, vc_ref[...])
    _fill_vaug(vaug, HALF + qb, vn_ref[...])
    n_sub = qb // DIL_SUB
    first_sub = pl.program_id(2) * n_sub

    for j in range(n_sub):
        gs = first_sub + j
        variant = jnp.where(gs == 0, 0, jnp.where(gs == n_sub_total - 1, 2, 1))
        r0 = j * DIL_SUB
        for hp in range(q_ref.shape[1] // LANES):
            o, lse = _attend_pair(q_ref[r0:r0 + DIL_SUB, LANES * hp:LANES * (hp + 1)],
                                  kbuf[r0:r0 + 2 * DIL_SUB, LANES * hp:LANES * (hp + 1)],
                                  vaug[r0:r0 + 2 * DIL_SUB, 2 * LANES * hp:2 * LANES * (hp + 1)],
                                  bias_ref[variant])
            o_ref[r0:r0 + DIL_SUB, LANES * hp:LANES * (hp + 1)] = o.astype(BF16)
            lse_ref[r0:r0 + DIL_SUB, LANES * hp:LANES * (hp + 1)] = lse


def _dilated(q, k, v, bias, dil):
    b, ln = q.shape[0], q.shape[1]
    qb = min(DIL_QB, ln)
    classes = min(dil, DIL_QB // qb)
    w = classes * (q.shape[2] // dil)
    hb = qb // HALF
    n_halo = ln // HALF
    main = pl.BlockSpec((None, qb, w), lambda bi, r, i: (bi, i, r))
    prev = pl.BlockSpec((None, HALF, w), lambda bi, r, i: (bi, jnp.maximum(i * hb - 1, 0), r))
    nxt = pl.BlockSpec((None, HALF, w), lambda bi, r, i: (bi, jnp.minimum((i + 1) * hb, n_halo - 1), r))
    o, lse = pl.pallas_call(
        functools.partial(_dilated_kernel, n_sub_total=ln // DIL_SUB),
        grid=(b, dil // classes, ln // qb),
        in_specs=[main, prev, main, nxt, prev, main, nxt, _const_spec(bias.shape)],
        out_specs=[main, main],
        out_shape=[jax.ShapeDtypeStruct(q.shape, BF16), jax.ShapeDtypeStruct(q.shape, F32)],
        scratch_shapes=[pltpu.VMEM((qb + 2 * HALF, w), BF16),
                        pltpu.VMEM((qb + 2 * HALF, 2 * w), BF16)],
        compiler_params=_cparams(3),
        name=f"dilated{dil}",
    )(q, k, k, k, v, v, v, bias)
    return o, lse


def _band_bias():
    row = np.arange(DIL_SUB)[:, None]
    col = np.arange(2 * DIL_SUB)[None, :]
    band = (col - row >= 0) & (col - row <= 2 * HALF)
    variants = [band & (col >= HALF), band, band & (col < 2 * DIL_SUB - HALF)]
    tab = np.stack([np.where(np.concatenate([m, m], axis=0), 0.0, NEG) for m in variants])
    return jnp.asarray(tab, dtype=F32)


NA_GROUP = 32
NA_TOK = NA_GROUP * GRID_W
NA_KEYS = NA_ROWS * GRID_W


def _natten_kernel(q_ref, kp_ref, kc_ref, kn_ref, vp_ref, vc_ref, vn_ref, bias_ref, qm_ref, km_ref, vm_ref,
                   o_ref, om_ref, kbuf, vaug, *, n_rows):
    kbuf[0:NA_TOK] = kp_ref[...]
    kbuf[NA_TOK:2 * NA_TOK] = kc_ref[...]
    kbuf[2 * NA_TOK:3 * NA_TOK] = kn_ref[...]
    pl.when(_first_step(2))(lambda: _fill_ones(vaug))
    _fill_vaug(vaug, 0, vp_ref[...], ones=False)
    _fill_vaug(vaug, NA_TOK, vc_ref[...], ones=False)
    _fill_vaug(vaug, 2 * NA_TOK, vn_ref[...], ones=False)
    g = pl.program_id(1)

    for j in range(NA_GROUP):
        r = g * NA_GROUP + j
        r0 = jnp.clip(r - NA_ROWS // 2, 0, n_rows - NA_ROWS)
        off = r - r0
        start = pl.multiple_of((r0 - (g - 1) * NA_GROUP) * GRID_W, GRID_W)
        q0 = j * GRID_W
        for hp in range(WIDTH_B // LANES):
            o, _ = _attend_pair(q_ref[q0:q0 + GRID_W, LANES * hp:LANES * (hp + 1)],
                                kbuf[pl.ds(start, NA_KEYS), LANES * hp:LANES * (hp + 1)],
                                vaug[pl.ds(start, NA_KEYS), 2 * LANES * hp:2 * LANES * (hp + 1)],
                                bias_ref[off, hp])
            o_ref[q0:q0 + GRID_W, LANES * hp:LANES * (hp + 1)] = o.astype(BF16)
        if j % 2 == 1:
            m0 = (j - 1) * GRID_W
            for hp in range(WIDTH_M // LANES):
                o, _ = _attend_pair(qm_ref[m0:m0 + 2 * GRID_W, LANES * hp:LANES * (hp + 1)],
                                    km_ref[:, LANES * hp:LANES * (hp + 1)],
                                    vm_ref[:, 2 * LANES * hp:2 * LANES * (hp + 1)], None)
                om_ref[m0:m0 + 2 * GRID_W, LANES * hp:LANES * (hp + 1)] = o.astype(BF16)


def _natten(q, k, v, bias, qm, km, vm):
    b, t, w = q.shape
    n_rows = t // GRID_W
    n_groups = n_rows // NA_GROUP
    blk = lambda f: pl.BlockSpec((None, NA_TOK, w), lambda bi, g: (bi, f(g), 0))
    cur = blk(lambda g: g)
    prev = blk(lambda g: jnp.maximum(g - 1, 0))
    nxt = blk(lambda g: jnp.minimum(g + 1, n_groups - 1))
    cur_m = pl.BlockSpec((None, NA_TOK, WIDTH_M), lambda bi, g: (bi, g, 0))
    per_b = lambda width: pl.BlockSpec((None, N_MEM, width), lambda bi, g: (bi, 0, 0))
    return pl.pallas_call(
        functools.partial(_natten_kernel, n_rows=n_rows),
        grid=(b, n_groups),
        in_specs=[cur, prev, cur, nxt, prev, cur, nxt, _const_spec(bias.shape),
                  cur_m, per_b(WIDTH_M), per_b(2 * WIDTH_M)],
        out_specs=[cur, cur_m],
        out_shape=[jax.ShapeDtypeStruct((b, t, w), BF16), jax.ShapeDtypeStruct((b, t, WIDTH_M), BF16)],
        scratch_shapes=[pltpu.VMEM((3 * NA_TOK, w), BF16), pltpu.VMEM((3 * NA_TOK, 2 * w), BF16)],
        compiler_params=_cparams(2, in_order=True),
        name="natten",
    )(q, k, k, k, v, v, v, bias, qm, km, vm)


def _natten_bias(rpb):
    c = np.arange(GRID_W)[:, None]
    kc = np.arange(GRID_W)[None, :]
    c0 = np.clip(c - NA_COLS // 2, 0, GRID_W - NA_COLS)
    valid = (kc >= c0) & (kc < c0 + NA_COLS)
    dcol = kc - c + NA_COLS - 1
    col_sel = (valid[..., None] & (dcol[..., None] == np.arange(2 * NA_COLS - 1))).astype(np.float32)
    cols = jnp.einsum("hrd,ckd->hcrk", rpb.astype(F32), col_sel, precision=lax.Precision.HIGHEST)
    cols = jnp.where(valid[None, :, None, :], cols * LOG2E, NEG)
    cols = cols.reshape(N_HEADS_B // 2, 2 * GRID_W, (2 * NA_ROWS - 1) * GRID_W)
    tab = jnp.stack([cols[:, :, (NA_ROWS - 1 - off) * GRID_W:(2 * NA_ROWS - 1 - off) * GRID_W]
                     for off in range(NA_ROWS)])
    return tab


FIN_TM = 512
FF_CHUNK = 1024


def _final_kernel(x_ref, o1_ref, o4_ref, o16_ref, l1_ref, l4_ref, l16_ref, ob_ref, om_ref,
                  ga_ref, gb_ref, gm_ref, wo_ref, gf_ref, w1_ref, w2_ref, y_ref,
                  *stages):
    tm = x_ref.shape[0]
    n_tiles = WIDTH_A // LANES
    rows4, rows16 = tm // DIL_STEP, tm // (DIL_STEP * DIL_STEP)

    def token_order(ref, stage, stage_b=None):
        src = lambda col: ref[:, col:col + LANES].astype(F32)
        if stage_b is not None:
            for r4 in range(DIL_STEP):
                for q in range(DIL_STEP):
                    for c in range(n_tiles):
                        stage_b[c, pl.ds(r4 * rows4 + q, rows16, stride=DIL_STEP), :] = (
                            src((r4 + DIL_STEP * q) * WIDTH_A + c * LANES))
        for r4 in range(DIL_STEP):
            for c in range(n_tiles):
                if stage_b is not None:
                    cls = stage_b[c, r4 * rows4:(r4 + 1) * rows4, :]
                else:
                    cls = src(r4 * WIDTH_A + c * LANES)
                stage[c, pl.ds(r4, rows4, stride=DIL_STEP), :] = cls
        return jnp.concatenate([stage[c] for c in range(n_tiles)], axis=1)

    o1, l1 = o1_ref[...].astype(F32), l1_ref[...]
    o4 = token_order(o4_ref, stages[0])
    l4 = token_order(l4_ref, stages[1])
    o16 = token_order(o16_ref, stages[2], stages[4])
    l16 = token_order(l16_ref, stages[3], stages[5])
    mx = jnp.maximum(jnp.maximum(l1, l4), l16)
    e1, e4, e16 = jnp.exp(l1 - mx), jnp.exp(l4 - mx), jnp.exp(l16 - mx)
    oa = (e1 * o1 + e4 * o4 + e16 * o16) / (e1 + e4 + e16)
    mixed = jnp.concatenate([_rms(oa, ga_ref[...]), _rms(ob_ref[...].astype(F32), gb_ref[...]),
                             _rms(om_ref[...].astype(F32), gm_ref[...])], axis=1).astype(BF16)
    x1 = x_ref[...] + jnp.dot(mixed, wo_ref[...], preferred_element_type=F32)
    hf = _rms(x1, gf_ref[...]).astype(BF16)
    acc = x1
    for c in range(0, D_FF, FF_CHUNK):
        h = jnp.dot(hf, w1_ref[:, c:c + FF_CHUNK], preferred_element_type=F32)
        h = jnp.square(jnp.maximum(h, 0.0)).astype(BF16)
        acc = acc + jnp.dot(h, w2_ref[c:c + FF_CHUNK, :], preferred_element_type=F32)
    y_ref[...] = acc


def _final(x, o1, o4, o16, l1, l4, l16, ob, om, ga, gb, gm, wo, gf, w1, w2):
    b, t, _ = x.shape
    tm = FIN_TM
    tok = lambda width: pl.BlockSpec((None, tm, width), lambda bi, i: (bi, i, 0))
    cls = [pl.BlockSpec((None, tm // d, d * WIDTH_A), lambda bi, i: (bi, i, 0)) for d in DILATIONS]
    return pl.pallas_call(
        _final_kernel,
        grid=(b, t // tm),
        in_specs=[tok(D_MODEL), *cls, *cls, tok(WIDTH_B), tok(WIDTH_M),
                  _const_spec((1, WIDTH_A)), _const_spec((1, WIDTH_B)), _const_spec((1, WIDTH_M)),
                  _const_spec((D_MODEL, D_MODEL)), _const_spec((1, D_MODEL)),
                  _const_spec((D_MODEL, D_FF)), _const_spec((D_FF, D_MODEL))],
        out_specs=tok(D_MODEL),
        out_shape=jax.ShapeDtypeStruct((b, t, D_MODEL), F32),
        scratch_shapes=[pltpu.VMEM((WIDTH_A // LANES, tm, LANES), F32)] * 6,
        compiler_params=_cparams(2),
        name="final",
    )(x, o1, o4, o16, l1, l4, l16, ob, om, ga, gb, gm, wo, gf, w1, w2)


def _rope_tables(t):
    half = ROPE_DIMS // 2
    inv = ROPE_THETA ** (-(np.arange(half, dtype=np.float64) * 2.0 / ROPE_DIMS))
    ang = np.arange(t, dtype=np.float64)[:, None] * inv[None, :]
    cs = jnp.asarray(np.concatenate([np.cos(ang), np.sin(ang)], axis=1), dtype=F32)
    cos, sin = cs[:, :half], cs[:, half:]
    rest = HEAD_DIM - ROPE_DIMS
    cos_h = jnp.concatenate([cos, cos, jnp.ones((t, rest), F32)], axis=1)
    sin_h = jnp.concatenate([-sin, sin, jnp.zeros((t, rest), F32)], axis=1)
    return jnp.tile(cos_h, (1, LANES // HEAD_DIM)), jnp.tile(sin_h, (1, LANES // HEAD_DIM))


def _group_columns(w):
    a, b = WIDTH_A, WIDTH_B
    qa, ka, va = w[:, :a], w[:, a:2 * a], w[:, 2 * a:3 * a]
    qb, kb, vb = w[:, 3 * a:3 * a + b], w[:, 3 * a + b:3 * a + 2 * b], w[:, 3 * a + 2 * b:3 * a + 3 * b]
    return jnp.concatenate([qa, ka, qb, kb, va, vb, w[:, 3 * a + 3 * b:]], axis=1)


def _layer(x, mem, p, tables):
    cos_t, sin_t = tables[x.shape[1]]
    qa, ka, va, (qb, kb, vb, qm) = _proj(x, p["g_attn"], p["w_in"], p["bd"], p["gqka"], p["gqkb"],
                                         p["gqm"], cos_t, sin_t)
    km, vm = _memkv(mem, p["g_mem"], p["w_kv"], p["bd"], p["gkm"])
    pats = [_dilated(qa[n], ka[n], va[n], p["band"], d) for n, d in enumerate(DILATIONS)]
    ob, om = _natten(qb, kb, vb, p["na_bias"], qm, km, vm)
    return _final(x, pats[0][0], pats[1][0], pats[2][0], pats[0][1], pats[1][1], pats[2][1],
                  ob, om, p["ga"], p["gb"], p["gm"], p["w_out"], p["g_ffn"], p["w1"], p["w2"])


def kernel(x_prompt, x_sample, mem_prompt, mem_sample, norm_attn, w_in, q_norm_a, k_norm_a, q_norm_b, k_norm_b, rpb_b, norm_mem, w_mem_kv, q_norm_m, k_norm_m, out_norm_a, out_norm_b, out_norm_m, w_out, norm_ffn, w_ff1, w_ff2):
    depth = w_in.shape[0]
    scale = HEAD_DIM ** -0.5 * LOG2E
    row = lambda v: v.astype(F32)[None, :]
    heads = lambda v, n: jnp.tile(v.astype(F32), n)[None, :]
    idx = np.arange(MXU_TILE)
    bd = jnp.asarray(idx[:, None] // HEAD_DIM == idx[None, :] // HEAD_DIM, dtype=BF16)
    band = _band_bias()
    tables = {t: _rope_tables(t) for t in {x_prompt.shape[1], x_sample.shape[1]}}
    y_prompt, y_sample = x_prompt, x_sample
    for i in range(depth):
        p = dict(
            g_attn=row(norm_attn[i]), w_in=_group_columns(w_in[i]).astype(BF16), bd=bd, band=band,
            gqka=jnp.concatenate([heads(q_norm_a[i], N_HEADS_A) * scale, heads(k_norm_a[i], N_HEADS_A)], axis=1),
            gqkb=jnp.concatenate([heads(q_norm_b[i], N_HEADS_B) * scale, heads(k_norm_b[i], N_HEADS_B)], axis=1),
            gqm=heads(q_norm_m[i], N_HEADS_M) * scale, gkm=heads(k_norm_m[i], N_HEADS_M),
            na_bias=_natten_bias(rpb_b[i]),
            g_mem=row(norm_mem[i]), w_kv=w_mem_kv[i].astype(BF16),
            ga=row(out_norm_a[i]), gb=row(out_norm_b[i]), gm=row(out_norm_m[i]),
            w_out=w_out[i].astype(BF16), g_ffn=row(norm_ffn[i]),
            w1=w_ff1[i].astype(BF16), w2=w_ff2[i].astype(BF16))
        y_prompt = _layer(y_prompt, mem_prompt, p, tables)
        y_sample = _layer(y_sample, mem_sample, p, tables)
    return (y_prompt, y_sample)
```
